```python
import math
import jax
import jax.numpy as jnp
from jax import lax
import numpy as np

D_MODEL = 2048
BATCH = 2
SEQ = 16384
DEPTH = 4

GRID_W = 64
CTX_LEN = 256

D_MIX = D_MODEL
W_LRU = D_MIX // 4
W_POOL = D_MIX // 4
W_FFT = D_MIX // 4
W_ATT = D_MIX // 4
LRU_BLOCKS = 8
LRU_BW = W_LRU // LRU_BLOCKS
LRU_CONV = 4
LRU_PAD = (2, 1)
LRU_C = 8.0
POOL_GROUPS = 4
POOL_GW = W_POOL // POOL_GROUPS
POOL_HALF = (1, 2, 4, 8)
FFT_HEADS = 4
FFT_HW = W_FFT // FFT_HEADS
ATT_HEADS = 4
ATT_DV = W_ATT // ATT_HEADS
ATT_DH = ATT_DV // 2
N_FREQ = ATT_DH // 4
ROPE_BASE = 10000.0
Q_BLOCK = 128
D_FF = 5632
FFN_CONV = 3
FFN_PAD = (1, 1)
N_MOD = 6
EPS = 1e-6
D_IN = 2 * W_LRU + W_POOL + W_FFT + 3 * W_ATT
SPLITS = (W_LRU, 2 * W_LRU, 2 * W_LRU + W_POOL, 2 * W_LRU + W_POOL + W_FFT,
          2 * W_LRU + W_POOL + W_FFT + W_ATT, 2 * W_LRU + W_POOL + W_FFT + 2 * W_ATT)
OFF_K = D_IN - 2 * W_ATT

kernel_name = 'hybrid_parallel_groups_diffusion_trunk'


def rmsnorm(x, g):
    xf = x.astype(jnp.float32)
    y = xf * lax.rsqrt(jnp.mean(xf * xf, axis=-1, keepdims=True) + EPS)
    return (y * g.astype(jnp.float32)).astype(x.dtype)


def modulate(h, shift, scale):
    return h * (1 + scale[:, None, :]) + shift[:, None, :]


def dwconv(x, w, b, pad):
    y = lax.conv_general_dilated(x, w[:, None, :].astype(x.dtype), window_strides=(1,), padding=[pad],
                                 dimension_numbers=('NWC', 'WIO', 'NWC'), feature_group_count=x.shape[-1])
    return y + b.astype(x.dtype)


def _lin_combine(e1, e2):
    a1, b1 = e1
    a2, b2 = e2
    return a1 * a2, a2 * b1 + b2


def rglru_scan(u, wa, ba, wi, bi, lam, h0, reverse):
    B_, L, _ = u.shape
    ub = u.reshape(B_, L, LRU_BLOCKS, LRU_BW)
    r = jax.nn.sigmoid(jnp.einsum('blnd,nde->blne', ub, wa).reshape(B_, L, W_LRU) + ba)
    i = jax.nn.sigmoid(jnp.einsum('blnd,nde->blne', ub, wi).reshape(B_, L, W_LRU) + bi)
    log_a = -LRU_C * r * jax.nn.softplus(-lam.astype(jnp.float32))
    a = jnp.exp(log_a)
    b = jnp.sqrt(-jnp.expm1(2.0 * log_a)) * (i * u)
    if h0 is not None:
        first = L - 1 if reverse else 0
        b = b.at[:, first].add(a[:, first] * h0)
    _, h = lax.associative_scan(_lin_combine, (a, b), axis=1, reverse=reverse)
    return h


def bidir_lru(ux, conv_w, conv_b, wa, ba, wi, bi, lam, h0f, h0b):
    u = dwconv(ux, conv_w, conv_b, LRU_PAD).astype(jnp.float32)
    hf = rglru_scan(u, wa[0], ba[0], wi[0], bi[0], lam[0], h0f, False)
    hb = rglru_scan(u, wa[1], ba[1], wi[1], bi[1], lam[1], h0b, True)
    return hf, hb


def pool_mixer(u, w_p, s_p):
    B_, L, _ = u.shape
    ug = u.astype(jnp.float32).reshape(B_, L, POOL_GROUPS, POOL_GW)
    cs = jnp.concatenate([jnp.zeros((B_, 1, POOL_GROUPS, POOL_GW), jnp.float32), jnp.cumsum(ug, axis=1)], axis=1)
    t = jnp.arange(L)[:, None]
    half = jnp.array(POOL_HALF, dtype=jnp.int32)[None, :]
    lo = jnp.clip(t - half, 0, L - 1)
    hi = jnp.clip(t + half - 1, 0, L - 1)

    def window_sum(cs_g, lo_g, hi_g):
        return cs_g[:, hi_g + 1] - cs_g[:, lo_g]

    s = jax.vmap(window_sum, in_axes=(2, 1, 1), out_axes=2)(cs, lo, hi)
    cnt = (hi - lo + 1).astype(jnp.float32)[None, :, :, None]
    d = s / cnt - ug
    y = jnp.einsum('blgc,gce->blge', d, w_p) * s_p.reshape(POOL_GROUPS, POOL_GW)
    return y.reshape(B_, L, W_POOL)


def fourier_mixer(u, w_f):
    B_, L, _ = u.shape
    uf = u.astype(jnp.float32).reshape(B_, L, FFT_HEADS, FFT_HW)
    y = jnp.fft.fftn(uf, axes=(1, 3), norm='ortho').real.reshape(B_, L, W_FFT)
    return y @ w_f


def qk_heads(t):
    return t.reshape(t.shape[0], t.shape[1], ATT_HEADS, 2, ATT_DH).astype(jnp.float32)


def v_heads(t):
    return t.reshape(t.shape[0], t.shape[1], ATT_HEADS, ATT_DV).astype(jnp.float32)


def _rot(x, ang):
    x1, x2 = jnp.split(x, 2, axis=-1)
    cos = jnp.cos(ang)[None, :, None, None, :]
    sin = jnp.sin(ang)[None, :, None, None, :]
    return jnp.concatenate([x1 * cos - x2 * sin, x2 * cos + x1 * sin], axis=-1)


def axial_rope(x, ang_r, ang_c):
    half = ATT_DH // 2
    return jnp.concatenate([_rot(x[..., :half], ang_r), _rot(x[..., half:], ang_c)], axis=-1)


def diff_attention(q, k, v, lam):
    B_, Lq = q.shape[0], q.shape[1]
    nb = Lq // Q_BLOCK
    qb = jnp.moveaxis(q.reshape(B_, nb, Q_BLOCK, ATT_HEADS, 2, ATT_DH), 1, 0)

    def block(qi):
        s = jnp.einsum('bqhjd,bkhjd->bhjqk', qi, k) * (ATT_DH ** -0.5)
        p = jax.nn.softmax(s, axis=-1)
        w = p[:, :, 0] - lam * p[:, :, 1]
        return jnp.einsum('bhqk,bkhe->bqhe', w, v)

    o = lax.map(block, qb)
    return jnp.moveaxis(o, 0, 1).reshape(B_, Lq, ATT_HEADS, ATT_DV)


def merge_groups(y_lru, y_pool, y_fft, o_att, g_sub, lam_init, w_o, dt):
    B_, L = y_pool.shape[0], y_pool.shape[1]
    y_att = (rmsnorm(o_att, g_sub) * (1.0 - lam_init)).reshape(B_, L, W_ATT)
    y = jnp.concatenate([y_lru.astype(dt), y_pool.astype(dt), y_fft.astype(dt), y_att.astype(dt)], axis=-1)
    return y @ w_o


def conv_ffn(h, w_up, conv_w, conv_b, w_down):
    g, v = jnp.split(h @ w_up, 2, axis=-1)
    g = dwconv(g, conv_w, conv_b, FFN_PAD)
    return (jax.nn.gelu(g) * v) @ w_down


def setup_inputs(seed: int = 0) -> dict:
    key = jax.random.key(seed)
    ks = jax.random.split(key, 26)
    f32 = jnp.float32

    def nrm(k, shape, fan_in, gain=1.0):
        return gain * (fan_in ** -0.5) * jax.random.normal(k, shape, f32)

    def small(k, shape, s=0.02):
        return s * jax.random.normal(k, shape, f32)

    a0 = jax.random.uniform(ks[13], (DEPTH, 2, W_LRU), f32, minval=0.9 ** (1.0 / LRU_C), maxval=0.999 ** (1.0 / LRU_C))
    return {
        'x': jax.random.normal(ks[0], (BATCH, SEQ, D_MODEL), f32),
        'c': jax.random.normal(ks[1], (BATCH, D_MODEL), f32),
        'ctx': jax.random.normal(ks[2], (BATCH, CTX_LEN, D_MODEL), f32),
        'c_ctx': jax.random.normal(ks[3], (D_MODEL,), f32),
        'ada_w': nrm(ks[4], (DEPTH, D_MODEL, N_MOD * D_MODEL), D_MODEL, 0.5),
        'ada_b': small(ks[5], (DEPTH, N_MOD * D_MODEL)),
        'norm_g': 1.0 + small(ks[6], (DEPTH, 4, D_MODEL), 0.05),
        'w_in': nrm(ks[7], (DEPTH, D_MODEL, D_IN), D_MODEL),
        'lru_conv_w': nrm(ks[8], (DEPTH, LRU_CONV, W_LRU), LRU_CONV),
        'lru_conv_b': small(ks[9], (DEPTH, W_LRU)),
        'lru_wa': nrm(ks[10], (DEPTH, 2, LRU_BLOCKS, LRU_BW, LRU_BW), LRU_BW),
        'lru_ba': small(ks[11], (DEPTH, 2, W_LRU)),
        'lru_wi': nrm(ks[12], (DEPTH, 2, LRU_BLOCKS, LRU_BW, LRU_BW), LRU_BW),
        'lru_bi': small(ks[14], (DEPTH, 2, W_LRU)),
        'lru_lam': jnp.log(a0) - jnp.log1p(-a0),
        'pool_w': nrm(ks[15], (DEPTH, POOL_GROUPS, POOL_GW, POOL_GW), POOL_GW),
        'pool_scale': 1.0 + small(ks[16], (DEPTH, W_POOL), 0.1),
        'fourier_w': nrm(ks[17], (DEPTH, W_FFT, W_FFT), W_FFT),
        'diff_lam': small(ks[18], (DEPTH, 4, ATT_DH), 0.1),
        'diff_subln_g': 1.0 + small(ks[19], (DEPTH, ATT_DV), 0.05),
        'w_out': nrm(ks[20], (DEPTH, D_MIX, D_MODEL), D_MIX),
        'ffn_w_up': nrm(ks[21], (DEPTH, D_MODEL, 2 * D_FF), D_MODEL),
        'ffn_conv_w': nrm(ks[22], (DEPTH, FFN_CONV, D_FF), FFN_CONV),
        'ffn_conv_b': small(ks[23], (DEPTH, D_FF)),
        'ffn_w_down': nrm(ks[24], (DEPTH, D_FF, D_MODEL), D_FF),
    }


def reference(x, c, ctx, c_ctx, ada_w, ada_b, norm_g, w_in, lru_conv_w, lru_conv_b, lru_wa, lru_ba, lru_wi,
              lru_bi, lru_lam, pool_w, pool_scale, fourier_w, diff_lam, diff_subln_g, w_out, ffn_w_up,
              ffn_conv_w, ffn_conv_b, ffn_w_down):
    f32 = jnp.float32
    dt = x.dtype
    B_, L, _ = x.shape
    rows = L // GRID_W
    t = jnp.arange(rows * GRID_W)
    row = (t // GRID_W).astype(f32)
    col = (t % GRID_W).astype(f32)
    inv = ROPE_BASE ** (-jnp.arange(N_FREQ, dtype=f32) / N_FREQ)
    ang_r = row[:, None] * inv
    ang_c = col[:, None] * inv

    silu_c = jax.nn.silu(c.astype(f32))
    silu_cc = jax.nn.silu(c_ctx.astype(f32))[None, :]
    xc = ctx
    for l in range(DEPTH):
        last = l == DEPTH - 1
        lam_init = 0.8 - 0.6 * math.exp(-0.3 * l)
        lam = (jnp.exp(jnp.sum(diff_lam[l, 0] * diff_lam[l, 1]).astype(f32))
               - jnp.exp(jnp.sum(diff_lam[l, 2] * diff_lam[l, 3]).astype(f32)) + lam_init)
        mod = (silu_c @ ada_w[l] + ada_b[l]).astype(dt)
        modc = (silu_cc @ ada_w[l] + ada_b[l]).astype(dt)
        sh1, sc1, g1, sh2, sc2, g2 = jnp.split(mod, N_MOD, axis=-1)
        csh1, csc1, cg1, csh2, csc2, cg2 = jnp.split(modc, N_MOD, axis=-1)
        lru_p = (lru_wa[l], lru_ba[l], lru_wi[l], lru_bi[l], lru_lam[l])

        hc = modulate(rmsnorm(xc, norm_g[l, 0]), csh1, csc1)
        if last:
            cx = hc @ w_in[l, :, :W_LRU]
            ck, cv = jnp.split(hc @ w_in[l, :, OFF_K:], 2, axis=-1)
        else:
            cx, cgate, cpool, cfft, cq, ck, cv = jnp.split(hc @ w_in[l], SPLITS, axis=-1)
        chf, chb = bidir_lru(cx, lru_conv_w[l], lru_conv_b[l], *lru_p, None, None)
        ck = qk_heads(ck)
        cv = v_heads(cv)

        h = modulate(rmsnorm(x, norm_g[l, 0]), sh1, sc1)
        ux, ugate, upool, ufft, q, k, v = jnp.split(h @ w_in[l], SPLITS, axis=-1)
        hf, hb = bidir_lru(ux, lru_conv_w[l], lru_conv_b[l], *lru_p, chf[:, -1], chb[:, 0])
        y_lru = jax.nn.gelu(ugate.astype(f32)) * (hf + hb)
        qr = axial_rope(qk_heads(q), ang_r, ang_c)
        kr = axial_rope(qk_heads(k), ang_r, ang_c)
        o = diff_attention(qr, jnp.concatenate([ck, kr], axis=1), jnp.concatenate([cv, v_heads(v)], axis=1), lam)
        y = merge_groups(y_lru, pool_mixer(upool, pool_w[l], pool_scale[l]), fourier_mixer(ufft, fourier_w[l]),
                         o, diff_subln_g[l], lam_init, w_out[l], dt)
        x = x + g1[:, None, :] * rmsnorm(y, norm_g[l, 1])

        if not last:
            yc = merge_groups(jax.nn.gelu(cgate.astype(f32)) * (chf + chb),
                              pool_mixer(cpool, pool_w[l], pool_scale[l]), fourier_mixer(cfft, fourier_w[l]),
                              diff_attention(qk_heads(cq), ck, cv, lam), diff_subln_g[l], lam_init, w_out[l], dt)
            xc = xc + cg1[:, None, :] * rmsnorm(yc, norm_g[l, 1])

        f = conv_ffn(modulate(rmsnorm(x, norm_g[l, 2]), sh2, sc2), ffn_w_up[l], ffn_conv_w[l], ffn_conv_b[l], ffn_w_down[l])
        x = x + g2[:, None, :] * rmsnorm(f, norm_g[l, 3])
        if not last:
            fc = conv_ffn(modulate(rmsnorm(xc, norm_g[l, 2]), csh2, csc2), ffn_w_up[l], ffn_conv_w[l], ffn_conv_b[l], ffn_w_down[l])
            xc = xc + cg2[:, None, :] * rmsnorm(fc, norm_g[l, 3])
    return x
```

```python
import functools
import math

import numpy as np
import jax
import jax.numpy as jnp
from jax import lax
from jax.experimental import pallas as pl
from jax.experimental.pallas import tpu as pltpu

F32 = jnp.float32
BF16 = jnp.bfloat16

EPS = 1e-6
N_MOD = 6
GROUP_W = 512
LRU_BLOCKS = 8
LRU_CONV = 4
LRU_C = 8.0
POOL_HALF = (1, 2, 4, 8)
HEADS = 4
HEAD_W = 128
ATT_DH = 64
N_FREQ = ATT_DH // 4
ROPE_BASE = 10000.0
GRID_W = 64
FFN_CONV = 3
SUBLANES = 8
BF16_ROWS = 16
DFT_N = 128
V7X_VMEM_LIMIT = 56 * 1024 * 1024
LOG2E = 1.4426950408889634


def _params(*sem):
    return pltpu.CompilerParams(dimension_semantics=sem, vmem_limit_bytes=V7X_VMEM_LIMIT)


def _rms(xf, g):
    return xf * lax.rsqrt(jnp.mean(xf * xf, axis=-1, keepdims=True) + EPS) * g


def _gelu(x):
    return x * (0.5 * (1.0 + jnp.tanh(0.7978845608028654 * (x + 0.044715 * (x * x * x)))))


def _sigmoid(x):
    return 1.0 / (1.0 + jnp.exp(-x))


def _ada_kernel(c_ref, w_ref, b_ref, o_ref):
    c = c_ref[...]
    s = (c * _sigmoid(c)).astype(BF16)
    o_ref[0] = jnp.dot(s, w_ref[0].astype(BF16), preferred_element_type=F32) + b_ref[0]


def _ada(c8, ada_w, ada_b):
    depth, d, nm = ada_w.shape
    tn = 1024
    return pl.pallas_call(
        _ada_kernel,
        grid=(depth, nm // tn),
        in_specs=[pl.BlockSpec((SUBLANES, d), lambda l, j: (0, 0)),
                  pl.BlockSpec((1, d, tn), lambda l, j: (l, 0, j)),
                  pl.BlockSpec((1, 1, tn), lambda l, j: (l, 0, j))],
        out_specs=pl.BlockSpec((1, SUBLANES, tn), lambda l, j: (l, 0, j)),
        out_shape=jax.ShapeDtypeStruct((depth, SUBLANES, nm), F32),
        compiler_params=_params("arbitrary", "arbitrary"),
        name="ada",
    )(c8, ada_w, ada_b.reshape(depth, 1, nm))


def _mod_spec(d, which, tiles_per_batch):
    if tiles_per_batch is None:
        return pl.BlockSpec((1, 1, d), lambda i, *_: (2 * N_MOD + which, 0, 0))
    return pl.BlockSpec((1, 1, d), lambda i, *_: ((i // tiles_per_batch) * N_MOD + which, 0, 0))


IN_SPLITS = (2 * GROUP_W, GROUP_W, GROUP_W, 3 * GROUP_W)


def _inproj_kernel(x_ref, sh_ref, sc_ref, g_ref, w_ref, *o_refs):
    h = _rms(x_ref[...], g_ref[...]) * (1.0 + sc_ref[0]) + sh_ref[0]
    p = jnp.dot(h.astype(BF16), w_ref[...], preferred_element_type=F32)
    off = 0
    for o_ref, wd in zip(o_refs, IN_SPLITS):
        o_ref[...] = p[:, off:off + wd]
        off += wd


def _in_proj(x2d, modl, ng, w_bf, tm, tiles_per_batch):
    n, d = x2d.shape
    d_in = w_bf.shape[1]
    return pl.pallas_call(
        _inproj_kernel,
        grid=(n // tm,),
        in_specs=[pl.BlockSpec((tm, d), lambda i: (i, 0)),
                  _mod_spec(d, 0, tiles_per_batch),
                  _mod_spec(d, 1, tiles_per_batch),
                  pl.BlockSpec((1, d), lambda i: (0, 0)),
                  pl.BlockSpec((d, d_in), lambda i: (0, 0), pipeline_mode=pl.Buffered(1))],
        out_specs=[pl.BlockSpec((tm, wd), lambda i: (i, 0)) for wd in IN_SPLITS],
        out_shape=[jax.ShapeDtypeStruct((n, wd), F32) for wd in IN_SPLITS],
        compiler_params=_params("arbitrary"),
        name="in_proj",
    )(x2d, modl, modl, ng, w_bf)


def _halo_specs(width, col, n_rows, seq, t, blk_of):
    per_b, per_t, last = seq // SUBLANES, t // SUBLANES, n_rows // SUBLANES - 1
    prev = pl.BlockSpec((SUBLANES, width),
                        lambda b, i, *_: (jnp.maximum(b * per_b + blk_of(i) * per_t - 1, 0), col))
    nxt = pl.BlockSpec((SUBLANES, width),
                       lambda b, i, *_: (jnp.minimum(b * per_b + (blk_of(i) + 1) * per_t, last), col))
    return prev, nxt


def _fill_ext(ext_s, x, prev, nxt, blk, nb, t):
    ext_s[0:SUBLANES, :] = jnp.where(blk == 0, 0.0, prev)
    ext_s[SUBLANES:SUBLANES + t, :] = x
    ext_s[SUBLANES + t:2 * SUBLANES + t, :] = jnp.where(blk == nb - 1, 0.0, nxt)


def _lru_kernel(*refs, t, nb, reverse, emit_y):
    if emit_y:
        (ux_ref, prev_ref, next_ref, gate_ref, hb_ref, cw_ref, cb_ref, wg_ref, bg_ref, lam_ref, h0_ref,
         out_ref, st_ref, ext_s, a_s, b_s, h_s, hin_s, car_s) = refs
    else:
        (ux_ref, prev_ref, next_ref, cw_ref, cb_ref, wg_ref, bg_ref, lam_ref, h0_ref,
         out_ref, st_ref, ext_s, a_s, b_s, h_s, hin_s, car_s) = refs
    i = pl.program_id(1)
    blk = (nb - 1 - i) if reverse else i
    w = GROUP_W
    ngrp = w // HEAD_W

    @pl.when(i == 0)
    def _():
        car_s[...] = h0_ref[0]

    _fill_ext(ext_s, ux_ref[...], prev_ref[...], next_ref[...], blk, nb, t)
    u = cb_ref[...] + cw_ref[0:1, :] * ext_s[pl.ds(SUBLANES - 2, t), :]
    for k in range(1, LRU_CONV):
        u = u + cw_ref[k:k + 1, :] * ext_s[pl.ds(SUBLANES - 2 + k, t), :]

    g = jnp.dot(u.astype(BF16), wg_ref[...], preferred_element_type=F32) + bg_ref[...]
    r = _sigmoid(g[:, :w])
    ig = _sigmoid(g[:, w:])
    nl = -lam_ref[...]
    softplus = jnp.maximum(nl, 0.0) + jnp.log1p(jnp.exp(-jnp.abs(nl)))
    log_a = (-LRU_C) * r * softplus
    a = jnp.exp(log_a)
    bb = jnp.sqrt(1.0 - a * a) * (ig * u)
    for cg in range(ngrp):
        a_s[cg] = a[:, cg * HEAD_W:(cg + 1) * HEAD_W]
        b_s[cg] = bb[:, cg * HEAD_W:(cg + 1) * HEAD_W]

    seg = t // SUBLANES
    order = list(range(seg))[::-1] if reverse else list(range(seg))
    seg_order = range(SUBLANES - 1, -1, -1) if reverse else range(SUBLANES)
    rows = lambda j: pl.ds(j, SUBLANES, stride=seg)
    c_all = car_s[...]
    c_new = []
    for cg in range(ngrp):
        hl = jnp.zeros((SUBLANES, HEAD_W), F32)
        pa = jnp.ones((SUBLANES, HEAD_W), F32)
        for j in order:
            aj = a_s[cg, rows(j), :]
            hl = aj * hl + b_s[cg, rows(j), :]
            pa = pa * aj
        c = c_all[:, cg * HEAD_W:(cg + 1) * HEAD_W]
        for s in seg_order:
            hin_s[cg, s:s + 1, :] = c
            c = pa[s:s + 1, :] * c + hl[s:s + 1, :]
        c_new.append(c)
        h = hin_s[cg]
        for j in order:
            h = a_s[cg, rows(j), :] * h + b_s[cg, rows(j), :]
            h_s[cg, rows(j), :] = h
    c = jnp.concatenate(c_new, axis=1)
    car_s[...] = c
    st_ref[0] = jnp.broadcast_to(c, (SUBLANES, w))
    hfull = jnp.concatenate([h_s[cg] for cg in range(ngrp)], axis=1)
    if emit_y:
        out_ref[...] = (_gelu(gate_ref[...]) * (hfull + hb_ref[...])).astype(out_ref.dtype)
    else:
        out_ref[...] = hfull


def _lru(p_lru, hb, cw, cb, wg, bg, lam, h0, batch, seq, t, reverse):
    n = p_lru.shape[0]
    w = GROUP_W
    nb = seq // t
    emit_y = not reverse
    blk_of = (lambda i: nb - 1 - i) if reverse else (lambda i: i)
    main = lambda col: pl.BlockSpec((t, w), lambda b, i: (b * nb + blk_of(i), col))
    prev, nxt = _halo_specs(w, 0, n, seq, t, blk_of)
    const = lambda shape: pl.BlockSpec(shape, lambda b, i: (0,) * len(shape))
    in_specs = [main(0), prev, nxt]
    args = [p_lru, p_lru, p_lru]
    if emit_y:
        in_specs += [main(1), main(0)]
        args += [p_lru, hb]
    in_specs += [const((LRU_CONV, w)), const((1, w)), const((w, 2 * w)), const((1, 2 * w)), const((1, w)),
                 pl.BlockSpec((1, 1, w), lambda b, i: (b, 0, 0))]
    args += [cw, cb, wg, bg, lam, h0]
    grp = (w // HEAD_W, t, HEAD_W)
    scratch = [pltpu.VMEM((t + 2 * SUBLANES, w), F32), pltpu.VMEM(grp, F32), pltpu.VMEM(grp, F32),
               pltpu.VMEM(grp, F32), pltpu.VMEM((w // HEAD_W, SUBLANES, HEAD_W), F32), pltpu.VMEM((1, w), F32)]
    return pl.pallas_call(
        functools.partial(_lru_kernel, t=t, nb=nb, reverse=reverse, emit_y=emit_y),
        grid=(batch, nb),
        in_specs=in_specs,
        out_specs=[pl.BlockSpec((t, w), lambda b, i: (b * nb + blk_of(i), 0)),
                   pl.BlockSpec((1, SUBLANES, w), lambda b, i: (b, 0, 0))],
        out_shape=[jax.ShapeDtypeStruct((n, w), BF16 if emit_y else F32),
                   jax.ShapeDtypeStruct((batch, SUBLANES, w), F32)],
        scratch_shapes=scratch,
        compiler_params=_params("arbitrary", "arbitrary"),
        name="lru_fwd" if emit_y else "lru_bwd",
    )(*args)


def _pool_kernel(x_ref, prev_ref, next_ref, wp_ref, sp_ref, o_ref, ext_s, *, t, nb, seq):
    i = pl.program_id(1)
    x = x_ref[...]
    _fill_ext(ext_s, x, prev_ref[...], next_ref[...], i, nb, t)
    pos = i * t + lax.broadcasted_iota(jnp.int32, (t, HEAD_W), 0)
    for g, hw in enumerate(POOL_HALF):
        cols = slice(g * HEAD_W, (g + 1) * HEAD_W)
        s = ext_s[pl.ds(SUBLANES - hw, t), cols]
        for dlt in range(-hw + 1, hw):
            s = s + ext_s[pl.ds(SUBLANES + dlt, t), cols]
        cnt = (jnp.minimum(pos + hw - 1, seq - 1) - jnp.maximum(pos - hw, 0) + 1).astype(F32)
        dif = s / cnt - x[:, cols]
        y = jnp.dot(dif.astype(BF16), wp_ref[g], preferred_element_type=F32) * sp_ref[:, cols]
        o_ref[:, cols] = y.astype(o_ref.dtype)


def _pool(p_pool, wp_bf, sp, batch, seq, t):
    n, w = p_pool.shape
    nb = seq // t
    prev, nxt = _halo_specs(w, 0, n, seq, t, lambda i: i)
    return pl.pallas_call(
        functools.partial(_pool_kernel, t=t, nb=nb, seq=seq),
        grid=(batch, nb),
        in_specs=[pl.BlockSpec((t, w), lambda b, i: (b * nb + i, 0)), prev, nxt,
                  pl.BlockSpec(wp_bf.shape, lambda b, i: (0, 0, 0)),
                  pl.BlockSpec((1, w), lambda b, i: (0, 0))],
        out_specs=pl.BlockSpec((t, w), lambda b, i: (b * nb + i, 0)),
        out_shape=jax.ShapeDtypeStruct((n, w), BF16),
        scratch_shapes=[pltpu.VMEM((t + 2 * SUBLANES, w), F32)],
        compiler_params=_params("arbitrary", "arbitrary"),
        name="pool",
    )(p_pool, p_pool, p_pool, wp_bf, sp)


def _cos_sin(n):
    ang = 2.0 * np.pi * np.outer(np.arange(n), np.arange(n)) / n
    return np.cos(ang), np.sin(ang)


def _ffta_kernel(x_ref, f1_ref, twc_ref, tws_ref, wc_ref, o_ref, *, t1n):
    w = GROUP_W
    hh = jnp.dot(f1_ref[...], x_ref[0].astype(BF16), preferred_element_type=F32)
    hr, hi = hh[:DFT_N], hh[DFT_N:]
    twc, tws = twc_ref[0], tws_ref[0]
    parts_r, parts_i = [], []
    for t1 in range(t1n):
        cw, sw = twc[:, t1:t1 + 1], tws[:, t1:t1 + 1]
        ar, ai = hr[:, t1 * w:(t1 + 1) * w], hi[:, t1 * w:(t1 + 1) * w]
        br = ar * cw + ai * sw
        bi = ai * cw - ar * sw
        for h in range(HEADS):
            parts_r.append(br[:, h * HEAD_W:(h + 1) * HEAD_W])
            parts_i.append(bi[:, h * HEAD_W:(h + 1) * HEAD_W])
    ab = jnp.concatenate([jnp.concatenate(parts_r, axis=0), jnp.concatenate(parts_i, axis=0)], axis=1)
    g = jnp.dot(ab.astype(BF16), wc_ref[...], preferred_element_type=F32)
    for t1 in range(t1n):
        for h in range(HEADS):
            rows = slice((t1 * HEADS + h) * DFT_N, (t1 * HEADS + h + 1) * DFT_N)
            base = t1 * 2 * w + h * HEAD_W
            o_ref[0, :, base:base + HEAD_W] = g[rows, :HEAD_W]
            o_ref[0, :, base + w:base + w + HEAD_W] = g[rows, HEAD_W:]


def _fftb_kernel(g_ref, f3_ref, wf_ref, o_ref):
    w = GROUP_W
    blk = g_ref[0, 0]
    gg = jnp.concatenate([blk[:, :w], blk[:, w:]], axis=0).astype(BF16)
    y = jnp.dot(f3_ref[...], gg, preferred_element_type=F32)
    o_ref[0] = jnp.dot(y.astype(BF16), wf_ref[...], preferred_element_type=F32).astype(o_ref.dtype)


def _fft_big(p_fft, wf_bf, batch, seq):
    w = GROUP_W
    n1 = seq // DFT_N
    t1n = SUBLANES
    c1, s1 = _cos_sin(DFT_N)
    cn, sn = _cos_sin(n1)
    sc = 1.0 / math.sqrt(DFT_N)
    f1 = jnp.asarray(np.concatenate([c1, -s1], axis=0) * sc, BF16)
    wc = jnp.asarray(np.block([[c1, -s1], [s1, c1]]) * sc, BF16)
    f3 = jnp.asarray(np.concatenate([cn, sn], axis=1) / math.sqrt(n1), BF16)
    ang = 2.0 * np.pi * np.outer(np.arange(DFT_N), np.arange(n1)) / seq
    tw = lambda m: jnp.asarray(m.reshape(DFT_N, n1 // t1n, t1n).transpose(1, 0, 2), F32)
    g = pl.pallas_call(
        functools.partial(_ffta_kernel, t1n=t1n),
        grid=(batch, n1 // t1n),
        in_specs=[pl.BlockSpec((1, DFT_N, t1n * w), lambda b, j: (b, 0, j)),
                  pl.BlockSpec(f1.shape, lambda b, j: (0, 0)),
                  pl.BlockSpec((1, DFT_N, t1n), lambda b, j: (j, 0, 0)),
                  pl.BlockSpec((1, DFT_N, t1n), lambda b, j: (j, 0, 0)),
                  pl.BlockSpec(wc.shape, lambda b, j: (0, 0))],
        out_specs=pl.BlockSpec((1, DFT_N, t1n * 2 * w), lambda b, j: (b, 0, j)),
        out_shape=jax.ShapeDtypeStruct((batch, DFT_N, n1 * 2 * w), F32),
        compiler_params=_params("arbitrary", "arbitrary"),
        name="fft_a",
    )(p_fft.reshape(batch, DFT_N, n1 * w), f1, tw(np.cos(ang)), tw(np.sin(ang)), wc)
    y = pl.pallas_call(
        _fftb_kernel,
        grid=(batch, DFT_N),
        in_specs=[pl.BlockSpec((1, 1, n1, 2 * w), lambda b, k: (b, k, 0, 0)),
                  pl.BlockSpec(f3.shape, lambda b, k: (0, 0)),
                  pl.BlockSpec((w, w), lambda b, k: (0, 0))],
        out_specs=pl.BlockSpec((1, n1, w), lambda b, k: (b, 0, k)),
        out_shape=jax.ShapeDtypeStruct((batch, n1, DFT_N * w), BF16),
        compiler_params=_params("arbitrary", "arbitrary"),
        name="fft_b",
    )(g.reshape(batch, DFT_N, n1, 2 * w), f3, wf_bf)
    return y.reshape(batch * seq, w)


def _fftd_kernel(x_ref, cs_ref, fl_ref, wf_ref, o_ref):
    xb = x_ref[...].astype(BF16)
    pa, pb = [], []
    for h in range(HEADS):
        z = jnp.dot(xb[:, h * HEAD_W:(h + 1) * HEAD_W], cs_ref[...], preferred_element_type=F32)
        pa.append(z[:, :HEAD_W])
        pb.append(z[:, HEAD_W:])
    ab = jnp.concatenate([jnp.concatenate(pa, axis=1), jnp.concatenate(pb, axis=1)], axis=0).astype(BF16)
    y = jnp.dot(fl_ref[...], ab, preferred_element_type=F32)
    o_ref[...] = jnp.dot(y.astype(BF16), wf_ref[...], preferred_element_type=F32).astype(o_ref.dtype)


def _fft_dense(p_fft, wf_bf, batch, seq):
    w = GROUP_W
    c1, s1 = _cos_sin(DFT_N)
    cl, sl = _cos_sin(seq)
    cs = jnp.asarray(np.concatenate([c1, s1], axis=1) / math.sqrt(DFT_N), BF16)
    fl = jnp.asarray(np.concatenate([cl, -sl], axis=1) / math.sqrt(seq), BF16)
    return pl.pallas_call(
        _fftd_kernel,
        grid=(batch,),
        in_specs=[pl.BlockSpec((seq, w), lambda b: (b, 0)),
                  pl.BlockSpec(cs.shape, lambda b: (0, 0)),
                  pl.BlockSpec(fl.shape, lambda b: (0, 0)),
                  pl.BlockSpec((w, w), lambda b: (0, 0))],
        out_specs=pl.BlockSpec((seq, w), lambda b: (b, 0)),
        out_shape=jax.ShapeDtypeStruct((batch * seq, w), BF16),
        compiler_params=_params("arbitrary"),
        name="fft_dense",
    )(p_fft, cs, fl, wf_bf)


def _prep_kernel(*refs, use_rope, qscale):
    if use_rope:
        p_ref, cos_ref, sin_ref, qt_ref, k_ref, vt_ref = refs
    else:
        p_ref, qt_ref, k_ref, vt_ref = refs
    t = p_ref.shape[0]
    w = GROUP_W
    if use_rope:
        cos, sin = cos_ref[...], sin_ref[...]
        lane = lax.broadcasted_iota(jnp.int32, (t, HEAD_W), 1)
        first = (lane % (2 * N_FREQ)) < N_FREQ

        def rope(x):
            partner = jnp.where(first, pltpu.roll(x, HEAD_W - N_FREQ, 1), pltpu.roll(x, N_FREQ, 1))
            return x * cos + partner * sin
    else:
        rope = lambda x: x
    for h in range(HEADS):
        q = rope(p_ref[:, h * HEAD_W:(h + 1) * HEAD_W])
        k = rope(p_ref[:, w + h * HEAD_W:w + (h + 1) * HEAD_W])
        v = p_ref[:, 2 * w + h * HEAD_W:2 * w + (h + 1) * HEAD_W]
        qt_ref[0, h] = (q * qscale).T.astype(BF16)
        k_ref[0, h] = k.astype(BF16)
        vt_ref[0, h, 0] = v.T.astype(BF16)


def _attn_prep(p_qkv, cos, sin, batch, seq, t, use_rope):
    nb = seq // t
    in_specs = [pl.BlockSpec((t, 3 * GROUP_W), lambda b, i: (b * nb + i, 0))]
    args = [p_qkv]
    if use_rope:
        in_specs += [pl.BlockSpec((t, HEAD_W), lambda b, i: (i, 0))] * 2
        args += [cos, sin]
    return pl.pallas_call(
        functools.partial(_prep_kernel, use_rope=use_rope, qscale=ATT_DH ** -0.5 * LOG2E),
        grid=(batch, nb),
        in_specs=in_specs,
        out_specs=[pl.BlockSpec((1, HEADS, HEAD_W, t), lambda b, i: (b, 0, 0, i)),
                   pl.BlockSpec((1, HEADS, t, HEAD_W), lambda b, i: (b, 0, i, 0)),
                   pl.BlockSpec((1, HEADS, 1, HEAD_W, t), lambda b, i: (b, 0, i, 0, 0))],
        out_shape=[jax.ShapeDtypeStruct((batch, HEADS, HEAD_W, seq), BF16),
                   jax.ShapeDtypeStruct((batch, HEADS, seq, HEAD_W), BF16),
                   jax.ShapeDtypeStruct((batch, HEADS, nb, HEAD_W, t), BF16)],
        compiler_params=_params("arbitrary", "arbitrary"),
        name="attn_prep",
    )(*args)


def _attn_kernel(*refs, tk, nchunks, has_prefix, lam_init):
    if has_prefix:
        qt_ref, k_ref, vt_ref, kc_ref, vtc_ref, dl_ref, gs_ref, o_ref, acc_s = refs
    else:
        qt_ref, k_ref, vt_ref, dl_ref, gs_ref, o_ref, acc_s = refs
    qt = qt_ref[0, 0]
    tq = qt.shape[1]
    row = lax.broadcasted_iota(jnp.int32, qt.shape, 0)
    zero = jnp.zeros_like(qt)
    qs = (jnp.where(row < ATT_DH, qt, zero), jnp.where(row >= ATT_DH, qt, zero))
    acc_s[...] = jnp.zeros_like(acc_s)

    def step(kb, vtb, carry):
        out = []
        for j in range(2):
            m, l = carry[2 * j], carry[2 * j + 1]
            s = jnp.dot(kb, qs[j], preferred_element_type=F32)
            mn = jnp.maximum(m, jnp.max(s, axis=0, keepdims=True))
            alpha = jnp.exp2(m - mn)
            p = jnp.exp2(s - mn)
            l = alpha * l + jnp.sum(p, axis=0, keepdims=True)
            acc_s[j] = alpha * acc_s[j] + jnp.dot(vtb, p.astype(BF16), preferred_element_type=F32)
            out += [mn, l]
        return tuple(out)

    neg = jnp.full((1, tq), -1e30, F32)
    zl = jnp.zeros((1, tq), F32)
    carry = (neg, zl, neg, zl)
    if has_prefix:
        carry = step(kc_ref[0, 0], vtc_ref[0, 0, 0], carry)
    carry = lax.fori_loop(
        0, nchunks,
        lambda c, car: step(k_ref[0, 0, pl.ds(pl.multiple_of(c * tk, tk), tk), :], vt_ref[0, 0, c], car),
        carry)
    dl = dl_ref[...]
    lam = (jnp.exp(jnp.sum(dl[0:1] * dl[1:2], axis=-1, keepdims=True))
           - jnp.exp(jnp.sum(dl[2:3] * dl[3:4], axis=-1, keepdims=True)) + lam_init)
    ot = acc_s[0] / carry[1] - lam * (acc_s[1] / carry[3])
    o_ref[...] = (_rms(ot.T, gs_ref[...]) * (1.0 - lam_init)).astype(o_ref.dtype)


def _attn(qt, k, vt, prefix, dl, gs, batch, seq, tq, tk, lam_init):
    nq = seq // tq
    nchunks = seq // tk
    in_specs = [pl.BlockSpec((1, 1, HEAD_W, tq), lambda b, h, i: (b, h, 0, i)),
                pl.BlockSpec((1, 1, seq, HEAD_W), lambda b, h, i: (b, h, 0, 0)),
                pl.BlockSpec((1, 1, nchunks, HEAD_W, tk), lambda b, h, i: (b, h, 0, 0, 0))]
    args = [qt, k, vt]
    if prefix is not None:
        kc, vtc = prefix
        in_specs += [pl.BlockSpec((1, 1) + kc.shape[2:], lambda b, h, i: (b, h, 0, 0)),
                     pl.BlockSpec((1, 1) + vtc.shape[2:], lambda b, h, i: (b, h, 0, 0, 0))]
        args += [kc, vtc]
    in_specs += [pl.BlockSpec(dl.shape, lambda b, h, i: (0, 0)),
                 pl.BlockSpec((1, HEAD_W), lambda b, h, i: (0, 0))]
    args += [dl, gs]
    return pl.pallas_call(
        functools.partial(_attn_kernel, tk=tk, nchunks=nchunks, has_prefix=prefix is not None,
                          lam_init=lam_init),
        grid=(batch, HEADS, nq),
        in_specs=in_specs,
        out_specs=pl.BlockSpec((tq, HEAD_W), lambda b, h, i: (b * nq + i, h)),
        out_shape=jax.ShapeDtypeStruct((batch * seq, GROUP_W), BF16),
        scratch_shapes=[pltpu.VMEM((2, HEAD_W, tq), F32)],
        compiler_params=_params("arbitrary", "arbitrary", "arbitrary"),
        name="attn",
    )(*args)


def _outproj_kernel(yl_ref, yp_ref, yf_ref, ya_ref, w_ref, x_ref, g_ref, ng_ref, o_ref):
    w = GROUP_W
    y = jnp.dot(yl_ref[...], w_ref[0:w, :], preferred_element_type=F32)
    for j, r in enumerate((yp_ref, yf_ref, ya_ref), start=1):
        y = y + jnp.dot(r[...], w_ref[j * w:(j + 1) * w, :], preferred_element_type=F32)
    o_ref[...] = x_ref[...] + g_ref[0] * _rms(y, ng_ref[...])


def _out_proj(ys, w_bf, x2d, modl, ng, tm, tiles_per_batch):
    n, d = x2d.shape
    ytile = pl.BlockSpec((tm, GROUP_W), lambda i: (i, 0))
    return pl.pallas_call(
        _outproj_kernel,
        grid=(n // tm,),
        in_specs=[ytile, ytile, ytile, ytile,
                  pl.BlockSpec(w_bf.shape, lambda i: (0, 0), pipeline_mode=pl.Buffered(1)),
                  pl.BlockSpec((tm, d), lambda i: (i, 0)),
                  _mod_spec(d, 2, tiles_per_batch),
                  pl.BlockSpec((1, d), lambda i: (0, 0))],
        out_specs=pl.BlockSpec((tm, d), lambda i: (i, 0)),
        out_shape=jax.ShapeDtypeStruct((n, d), F32),
        compiler_params=_params("arbitrary"),
        name="out_proj",
    )(*ys, w_bf, x2d, modl, ng)


def _ffn_kernel(x_ref, prev_ref, next_ref, sh_ref, sc_ref, g2_ref, ng2_ref, ng3_ref, wg_ref, wv_ref, cw_ref,
                cb_ref, wd_ref, o_ref, hn_s, gs_s, *, tm, nb, nf):
    i = pl.program_id(0) % nb
    j = pl.program_id(1)
    hr = BF16_ROWS

    @pl.when(j == 0)
    def _():
        norm = lambda v: (_rms(v, ng2_ref[...]) * (1.0 + sc_ref[0]) + sh_ref[0]).astype(BF16)
        hn_s[0:hr, :] = norm(prev_ref[...])
        hn_s[hr:hr + tm, :] = norm(x_ref[...])
        hn_s[hr + tm:2 * hr + tm, :] = norm(next_ref[...])

    g = jnp.dot(hn_s[...], wg_ref[...], preferred_element_type=F32)
    gs_s[0:hr, :] = jnp.where(i == 0, 0.0, g[0:hr])
    gs_s[hr:hr + tm, :] = g[hr:hr + tm]
    gs_s[hr + tm:2 * hr + tm, :] = jnp.where(i == nb - 1, 0.0, g[hr + tm:2 * hr + tm])
    gc = cb_ref[...] + cw_ref[0:1, :] * gs_s[pl.ds(hr - 1, tm), :]
    for k in range(1, FFN_CONV):
        gc = gc + cw_ref[k:k + 1, :] * gs_s[pl.ds(hr - 1 + k, tm), :]
    v = jnp.dot(hn_s[hr:hr + tm, :], wv_ref[...], preferred_element_type=F32)
    part = jnp.dot((_gelu(gc) * v).astype(BF16), wd_ref[...], preferred_element_type=F32)

    @pl.when(j == 0)
    def _():
        o_ref[...] = part

    @pl.when(j > 0)
    def _():
        o_ref[...] += part

    @pl.when(j == nf - 1)
    def _():
        o_ref[...] = x_ref[...] + g2_ref[0] * _rms(o_ref[...], ng3_ref[...])


def _ffn(x2d, modl, ng2, ng3, wup_bf, cw, cb, wdn_bf, seq, tm, tf, tiles_per_batch):
    n, d = x2d.shape
    d_ff = wdn_bf.shape[0]
    nb, nf = seq // tm, d_ff // tf
    hr = BF16_ROWS
    per_t, last = tm // hr, n // hr - 1
    prev = pl.BlockSpec((hr, d), lambda i, j: (jnp.maximum(i * per_t - 1, 0), 0))
    nxt = pl.BlockSpec((hr, d), lambda i, j: (jnp.minimum((i + 1) * per_t, last), 0))
    vec = pl.BlockSpec((1, d), lambda i, j: (0, 0))
    return pl.pallas_call(
        functools.partial(_ffn_kernel, tm=tm, nb=nb, nf=nf),
        grid=(n // tm, nf),
        in_specs=[pl.BlockSpec((tm, d), lambda i, j: (i, 0)), prev, nxt,
                  _mod_spec(d, 3, tiles_per_batch), _mod_spec(d, 4, tiles_per_batch),
                  _mod_spec(d, 5, tiles_per_batch), vec, vec,
                  pl.BlockSpec((d, tf), lambda i, j: (0, j)),
                  pl.BlockSpec((d, tf), lambda i, j: (0, nf + j)),
                  pl.BlockSpec((FFN_CONV, tf), lambda i, j: (0, j)),
                  pl.BlockSpec((1, tf), lambda i, j: (0, j)),
                  pl.BlockSpec((tf, d), lambda i, j: (j, 0))],
        out_specs=pl.BlockSpec((tm, d), lambda i, j: (i, 0)),
        out_shape=jax.ShapeDtypeStruct((n, d), F32),
        scratch_shapes=[pltpu.VMEM((tm + 2 * hr, d), BF16), pltpu.VMEM((tm + 2 * hr, tf), F32)],
        compiler_params=_params("arbitrary", "arbitrary"),
        name="ffn",
    )(x2d, x2d, x2d, modl, modl, modl, ng2, ng3, wup_bf, wup_bf, cw, cb, wdn_bf)


def _rope_tables(seq):
    t = jnp.arange(seq)
    row = (t // GRID_W).astype(F32)
    col = (t % GRID_W).astype(F32)
    inv = ROPE_BASE ** (-jnp.arange(N_FREQ, dtype=F32) / N_FREQ)
    ang_r, ang_c = row[:, None] * inv, col[:, None] * inv
    cos = jnp.concatenate([jnp.cos(ang_r)] * 2 + [jnp.cos(ang_c)] * 2, axis=1)
    sin = jnp.concatenate([-jnp.sin(ang_r), jnp.sin(ang_r), -jnp.sin(ang_c), jnp.sin(ang_c)], axis=1)
    return jnp.tile(cos, (1, 2)), jnp.tile(sin, (1, 2))


def _gate_weights(wa, ba, wi, bi):
    eye = jnp.eye(LRU_BLOCKS, dtype=wa.dtype)
    dense = lambda m: jnp.einsum('nde,nm->ndme', m, eye).reshape(GROUP_W, GROUP_W)
    return (jnp.concatenate([dense(wa), dense(wi)], axis=1).astype(BF16),
            jnp.concatenate([ba, bi])[None, :])


def _row_tile(seq):
    return min(seq, 512)


def _mixers(p, layer, batch, seq, h0f, h0b, rope, prefix, lam_init, want_y):
    p_lru, p_pool, p_fft, p_qkv = p
    t = _row_tile(seq)
    hb, st_b = _lru(p_lru, None, layer['cw'], layer['cb'], layer['wg'][1], layer['bg'][1], layer['lam'][1],
                    h0b, batch, seq, t, True)
    y_lru, st_f = _lru(p_lru, hb, layer['cw'], layer['cb'], layer['wg'][0], layer['bg'][0], layer['lam'][0],
                       h0f, batch, seq, t, False)
    use_rope = rope is not None
    qt, k, vt = _attn_prep(p_qkv, *(rope if use_rope else (None, None)), batch, seq, t, use_rope)
    ys = None
    if want_y:
        y_pool = _pool(p_pool, layer['wp'], layer['sp'], batch, seq, t)
        if seq % (DFT_N * SUBLANES) == 0:
            y_fft = _fft_big(p_fft, layer['wf'], batch, seq)
        else:
            y_fft = _fft_dense(p_fft, layer['wf'], batch, seq)
        y_att = _attn(qt, k, vt, prefix, layer['dl'], layer['gs'], batch, seq, t, t, lam_init)
        ys = (y_lru, y_pool, y_fft, y_att)
    return ys, (st_f, st_b), (k, vt)


def kernel(x, c, ctx, c_ctx, ada_w, ada_b, norm_g, w_in, lru_conv_w, lru_conv_b, lru_wa, lru_ba, lru_wi, lru_bi,
           lru_lam, pool_w, pool_scale, fourier_w, diff_lam, diff_subln_g, w_out, ffn_w_up, ffn_conv_w,
           ffn_conv_b, ffn_w_down):
    batch, seq, d = x.shape
    ctx_len = ctx.shape[1]
    depth = ada_w.shape[0]
    d_ff = ffn_w_down.shape[1]
    tm, tmc = _row_tile(seq), _row_tile(ctx_len)
    tpb = seq // tm
    tf = 512

    c8 = jnp.concatenate([c.astype(F32), c_ctx.astype(F32)[None, :],
                          jnp.zeros((SUBLANES - batch - 1, d), F32)], axis=0)
    mod = _ada(c8, ada_w, ada_b).reshape(depth, SUBLANES * N_MOD, 1, d)
    rope = _rope_tables(seq)
    zeros_state = jnp.zeros((batch, 1, GROUP_W), F32)

    x2 = x.reshape(batch * seq, d)
    xc = ctx.reshape(batch * ctx_len, d)
    for l in range(depth):
        last = l == depth - 1
        lam_init = 0.8 - 0.6 * math.exp(-0.3 * l)
        gates = [_gate_weights(lru_wa[l, dr], lru_ba[l, dr], lru_wi[l, dr], lru_bi[l, dr]) for dr in range(2)]
        layer = dict(cw=lru_conv_w[l], cb=lru_conv_b[l][None, :],
                     wg=[g[0] for g in gates], bg=[g[1] for g in gates],
                     lam=[lru_lam[l, dr][None, :] for dr in range(2)],
                     wp=pool_w[l].astype(BF16), sp=pool_scale[l][None, :], wf=fourier_w[l].astype(BF16),
                     dl=diff_lam[l], gs=diff_subln_g[l][None, :])
        w_in_bf, w_out_bf = w_in[l].astype(BF16), w_out[l].astype(BF16)
        w_up_bf, w_dn_bf = ffn_w_up[l].astype(BF16), ffn_w_down[l].astype(BF16)
        ng = [norm_g[l, k][None, :] for k in range(4)]
        modl = mod[l]
        ffn = functools.partial(_ffn, ng2=ng[2], ng3=ng[3], wup_bf=w_up_bf, cw=ffn_conv_w[l],
                                cb=ffn_conv_b[l][None, :], wdn_bf=w_dn_bf, tf=tf)

        pc = _in_proj(xc, modl, ng[0], w_in_bf, tmc, None)
        ysc, (stf, stb), ctx_kv = _mixers(pc, layer, batch, ctx_len, zeros_state, zeros_state, None, None,
                                          lam_init, not last)
        pz = _in_proj(x2, modl, ng[0], w_in_bf, tm, tpb)
        ys, _, _ = _mixers(pz, layer, batch, seq, stf[:, :1], stb[:, :1], rope, ctx_kv, lam_init, True)
        x2 = _out_proj(ys, w_out_bf, x2, modl, ng[1], tm, tpb)
        x2 = ffn(x2, modl, seq=seq, tm=tm, tiles_per_batch=tpb)
        if not last:
            xc = _out_proj(ysc, w_out_bf, xc, modl, ng[1], tmc, None)
            xc = ffn(xc, modl, seq=ctx_len, tm=tmc, tiles_per_batch=None)
    return x2.reshape(batch, seq, d)
```

```python
import functools
import math

import numpy as np
import jax
import jax.numpy as jnp
from jax import lax
from jax.experimental import pallas as pl
from jax.experimental.pallas import tpu as pltpu

F32 = jnp.float32
BF16 = jnp.bfloat16

EPS = 1e-6
N_MOD = 6
GROUP_W = 512
LRU_BLOCKS = 8
LRU_CONV = 4
LRU_C = 8.0
SQRT_GUARD = 1e-37
POOL_HALF = (1, 2, 4, 8)
HEADS = 4
HEAD_W = 128
ATT_DH = 64
ATT_TK = 256
ATT_WIDE = 2
ATT_UNROLL = 3
N_FREQ = ATT_DH // 4
ROPE_BASE = 10000.0
GRID_W = 64
FFN_CONV = 3
SUBLANES = 8
BF16_ROWS = 16
DFT_N = 128
FFT_KB = 4
V7X_VMEM_LIMIT = 56 * 1024 * 1024
LOG2E = 1.4426950408889634


def _params(*sem):
    return pltpu.CompilerParams(dimension_semantics=sem, vmem_limit_bytes=V7X_VMEM_LIMIT)


def _rms(xf, g):
    return xf * lax.rsqrt(jnp.mean(xf * xf, axis=-1, keepdims=True) + EPS) * g


def _gelu(x):
    return x * (0.5 * (1.0 + jnp.tanh(0.7978845608028654 * (x + 0.044715 * (x * x * x)))))


def _sigmoid(x):
    return 1.0 / (1.0 + jnp.exp(-x))


def _ada_kernel(c_ref, w_ref, b_ref, o_ref):
    c = c_ref[...]
    s = (c * _sigmoid(c)).astype(BF16)
    o_ref[0] = jnp.dot(s, w_ref[0].astype(BF16), preferred_element_type=F32) + b_ref[0]


def _ada(c8, ada_w, ada_b):
    depth, d, nm = ada_w.shape
    tn = 1024
    return pl.pallas_call(
        _ada_kernel,
        grid=(depth, nm // tn),
        in_specs=[pl.BlockSpec((SUBLANES, d), lambda l, j: (0, 0)),
                  pl.BlockSpec((1, d, tn), lambda l, j: (l, 0, j)),
                  pl.BlockSpec((1, 1, tn), lambda l, j: (l, 0, j))],
        out_specs=pl.BlockSpec((1, SUBLANES, tn), lambda l, j: (l, 0, j)),
        out_shape=jax.ShapeDtypeStruct((depth, SUBLANES, nm), F32),
        compiler_params=_params("arbitrary", "arbitrary"),
        name="ada",
    )(c8, ada_w, ada_b.reshape(depth, 1, nm))


def _mod_spec(d, which, tiles_per_batch):
    if tiles_per_batch is None:
        return pl.BlockSpec((1, 1, d), lambda i, *_: (2 * N_MOD + which, 0, 0))
    return pl.BlockSpec((1, 1, d), lambda i, *_: ((i // tiles_per_batch) * N_MOD + which, 0, 0))


IN_SPLITS = (2 * GROUP_W, GROUP_W, GROUP_W, 3 * GROUP_W)


def _inproj_kernel(x_ref, sh_ref, sc_ref, g_ref, w_ref, *o_refs):
    h = _rms(x_ref[...], g_ref[...]) * (1.0 + sc_ref[0]) + sh_ref[0]
    p = jnp.dot(h.astype(BF16), w_ref[...], preferred_element_type=F32)
    off = 0
    for o_ref, wd in zip(o_refs, IN_SPLITS):
        o_ref[...] = p[:, off:off + wd]
        off += wd


def _in_proj(x2d, modl, ng, w_bf, tm, tiles_per_batch):
    n, d = x2d.shape
    d_in = w_bf.shape[1]
    return pl.pallas_call(
        _inproj_kernel,
        grid=(n // tm,),
        in_specs=[pl.BlockSpec((tm, d), lambda i: (i, 0)),
                  _mod_spec(d, 0, tiles_per_batch),
                  _mod_spec(d, 1, tiles_per_batch),
                  pl.BlockSpec((1, d), lambda i: (0, 0)),
                  pl.BlockSpec((d, d_in), lambda i: (0, 0), pipeline_mode=pl.Buffered(1))],
        out_specs=[pl.BlockSpec((tm, wd), lambda i: (i, 0)) for wd in IN_SPLITS],
        out_shape=[jax.ShapeDtypeStruct((n, wd), F32) for wd in IN_SPLITS],
        compiler_params=_params("arbitrary"),
        name="in_proj",
    )(x2d, modl, modl, ng, w_bf)


def _halo_specs(width, col, n_rows, seq, t, blk_of):
    per_b, per_t, last = seq // SUBLANES, t // SUBLANES, n_rows // SUBLANES - 1
    prev = pl.BlockSpec((SUBLANES, width),
                        lambda b, i, *_: (jnp.maximum(b * per_b + blk_of(i) * per_t - 1, 0), col))
    nxt = pl.BlockSpec((SUBLANES, width),
                       lambda b, i, *_: (jnp.minimum(b * per_b + (blk_of(i) + 1) * per_t, last), col))
    return prev, nxt


def _fill_ext(ext_s, x, prev, nxt, blk, nb, t):
    ext_s[0:SUBLANES, :] = jnp.where(blk == 0, 0.0, prev)
    ext_s[SUBLANES:SUBLANES + t, :] = x
    ext_s[SUBLANES + t:2 * SUBLANES + t, :] = jnp.where(blk == nb - 1, 0.0, nxt)


def _lru_kernel(*refs, t, nb, reverse, emit_y):
    if emit_y:
        (ux_ref, prev_ref, next_ref, gate_ref, hb_ref, cw_ref, cb_ref, wg_ref, bg_ref, lam_ref, h0_ref,
         out_ref, st_ref, ext_s, car_s) = refs
    else:
        (ux_ref, prev_ref, next_ref, cw_ref, cb_ref, wg_ref, bg_ref, lam_ref, h0_ref,
         out_ref, st_ref, ext_s, car_s) = refs
    i = pl.program_id(1)
    blk = (nb - 1 - i) if reverse else i
    w = GROUP_W
    ngrp = w // HEAD_W

    @pl.when(i == 0)
    def _():
        car_s[...] = h0_ref[0]

    _fill_ext(ext_s, ux_ref[...], prev_ref[...], next_ref[...], blk, nb, t)
    u = cb_ref[...] + cw_ref[0:1, :] * ext_s[pl.ds(SUBLANES - 2, t), :]
    for k in range(1, LRU_CONV):
        u = u + cw_ref[k:k + 1, :] * ext_s[pl.ds(SUBLANES - 2 + k, t), :]

    g = jnp.dot(u.astype(BF16), wg_ref[...], preferred_element_type=F32) + bg_ref[...]
    r = _sigmoid(g[:, :w])
    ig = _sigmoid(g[:, w:])
    nl = -lam_ref[...]
    softplus = jnp.maximum(nl, 0.0) + jnp.log1p(jnp.exp(-jnp.abs(nl)))
    log_a = (-LRU_C) * r * softplus
    a = jnp.exp(log_a)
    one_m = 1.0 - a * a
    bb = (one_m * lax.rsqrt(jnp.maximum(one_m, SQRT_GUARD))) * (ig * u)

    ng = t // SUBLANES
    rowid = lax.broadcasted_iota(jnp.int32, (1, SUBLANES, HEAD_W), 1)
    c_all = car_s[...]
    c_new, h_cols = [], []
    for cg in range(ngrp):
        av = a[:, cg * HEAD_W:(cg + 1) * HEAD_W].reshape(ng, SUBLANES, HEAD_W)
        bv = bb[:, cg * HEAD_W:(cg + 1) * HEAD_W].reshape(ng, SUBLANES, HEAD_W)
        for dist in (1, 2, 4):
            shift = SUBLANES - dist if reverse else dist
            valid = (rowid < SUBLANES - dist) if reverse else (rowid >= dist)
            b_far = jnp.where(valid, pltpu.roll(bv, shift, 1), 0.0)
            a_far = jnp.where(valid, pltpu.roll(av, shift, 1), 1.0)
            bv = bv + av * b_far
            av = av * a_far
        c = c_all[:, cg * HEAD_W:(cg + 1) * HEAD_W]
        hs = [None] * ng
        for g in (range(ng - 1, -1, -1) if reverse else range(ng)):
            hs[g] = bv[g] + av[g] * c
            c = hs[g][0:1, :] if reverse else hs[g][SUBLANES - 1:SUBLANES, :]
        c_new.append(c)
        h_cols.append(jnp.concatenate(hs, axis=0))
    c = jnp.concatenate(c_new, axis=1)
    car_s[...] = c
    st_ref[0] = jnp.broadcast_to(c, (SUBLANES, w))
    hfull = jnp.concatenate(h_cols, axis=1)
    if emit_y:
        out_ref[...] = (_gelu(gate_ref[...]) * (hfull + hb_ref[...])).astype(out_ref.dtype)
    else:
        out_ref[...] = hfull


def _lru(p_lru, hb, cw, cb, wg, bg, lam, h0, batch, seq, t, reverse):
    n = p_lru.shape[0]
    w = GROUP_W
    nb = seq // t
    emit_y = not reverse
    blk_of = (lambda i: nb - 1 - i) if reverse else (lambda i: i)
    main = lambda col: pl.BlockSpec((t, w), lambda b, i: (b * nb + blk_of(i), col))
    prev, nxt = _halo_specs(w, 0, n, seq, t, blk_of)
    const = lambda shape: pl.BlockSpec(shape, lambda b, i: (0,) * len(shape))
    in_specs = [main(0), prev, nxt]
    args = [p_lru, p_lru, p_lru]
    if emit_y:
        in_specs += [main(1), main(0)]
        args += [p_lru, hb]
    in_specs += [const((LRU_CONV, w)), const((1, w)), const((w, 2 * w)), const((1, 2 * w)), const((1, w)),
                 pl.BlockSpec((1, 1, w), lambda b, i: (b, 0, 0))]
    args += [cw, cb, wg, bg, lam, h0]
    scratch = [pltpu.VMEM((t + 2 * SUBLANES, w), F32), pltpu.VMEM((1, w), F32)]
    return pl.pallas_call(
        functools.partial(_lru_kernel, t=t, nb=nb, reverse=reverse, emit_y=emit_y),
        grid=(batch, nb),
        in_specs=in_specs,
        out_specs=[pl.BlockSpec((t, w), lambda b, i: (b * nb + blk_of(i), 0)),
                   pl.BlockSpec((1, SUBLANES, w), lambda b, i: (b, 0, 0))],
        out_shape=[jax.ShapeDtypeStruct((n, w), BF16 if emit_y else F32),
                   jax.ShapeDtypeStruct((batch, SUBLANES, w), F32)],
        scratch_shapes=scratch,
        compiler_params=_params("arbitrary", "arbitrary"),
        name="lru_fwd" if emit_y else "lru_bwd",
    )(*args)


def _pool_kernel(x_ref, prev_ref, next_ref, wp_ref, sp_ref, o_ref, ext_s, *, t, nb, seq):
    i = pl.program_id(1)
    x = x_ref[...]
    _fill_ext(ext_s, x, prev_ref[...], next_ref[...], i, nb, t)
    pos = i * t + lax.broadcasted_iota(jnp.int32, (t, HEAD_W), 0)
    for g, hw in enumerate(POOL_HALF):
        cols = slice(g * HEAD_W, (g + 1) * HEAD_W)
        s = ext_s[pl.ds(SUBLANES - hw, t), cols]
        for dlt in range(-hw + 1, hw):
            s = s + ext_s[pl.ds(SUBLANES + dlt, t), cols]
        cnt = (jnp.minimum(pos + hw - 1, seq - 1) - jnp.maximum(pos - hw, 0) + 1).astype(F32)
        dif = s / cnt - x[:, cols]
        y = jnp.dot(dif.astype(BF16), wp_ref[g], preferred_element_type=F32) * sp_ref[:, cols]
        o_ref[:, cols] = y.astype(o_ref.dtype)


def _pool(p_pool, wp_bf, sp, batch, seq, t):
    n, w = p_pool.shape
    nb = seq // t
    prev, nxt = _halo_specs(w, 0, n, seq, t, lambda i: i)
    return pl.pallas_call(
        functools.partial(_pool_kernel, t=t, nb=nb, seq=seq),
        grid=(batch, nb),
        in_specs=[pl.BlockSpec((t, w), lambda b, i: (b * nb + i, 0)), prev, nxt,
                  pl.BlockSpec(wp_bf.shape, lambda b, i: (0, 0, 0)),
                  pl.BlockSpec((1, w), lambda b, i: (0, 0))],
        out_specs=pl.BlockSpec((t, w), lambda b, i: (b * nb + i, 0)),
        out_shape=jax.ShapeDtypeStruct((n, w), BF16),
        scratch_shapes=[pltpu.VMEM((t + 2 * SUBLANES, w), F32)],
        compiler_params=_params("arbitrary", "arbitrary"),
        name="pool",
    )(p_pool, p_pool, p_pool, wp_bf, sp)


def _cos_sin(n):
    ang = 2.0 * np.pi * np.outer(np.arange(n), np.arange(n)) / n
    return np.cos(ang), np.sin(ang)


def _ffta_kernel(x_ref, f1_ref, twc_ref, tws_ref, wc_ref, o_ref, *, t1n):
    w = GROUP_W
    hh = jnp.dot(f1_ref[...], x_ref[0].astype(BF16), preferred_element_type=F32)
    hr, hi = hh[:DFT_N], hh[DFT_N:]
    twc, tws = twc_ref[0], tws_ref[0]
    parts_r, parts_i = [], []
    for t1 in range(t1n):
        cw, sw = twc[:, t1:t1 + 1], tws[:, t1:t1 + 1]
        ar, ai = hr[:, t1 * w:(t1 + 1) * w], hi[:, t1 * w:(t1 + 1) * w]
        br = ar * cw + ai * sw
        bi = ai * cw - ar * sw
        for h in range(HEADS):
            parts_r.append(br[:, h * HEAD_W:(h + 1) * HEAD_W])
            parts_i.append(bi[:, h * HEAD_W:(h + 1) * HEAD_W])
    ab = jnp.concatenate([jnp.concatenate(parts_r, axis=0), jnp.concatenate(parts_i, axis=0)], axis=1)
    g = jnp.dot(ab.astype(BF16), wc_ref[...], preferred_element_type=F32)
    for t1 in range(t1n):
        for h in range(HEADS):
            rows = slice((t1 * HEADS + h) * DFT_N, (t1 * HEADS + h + 1) * DFT_N)
            base = t1 * 2 * w + h * HEAD_W
            o_ref[0, :, base:base + HEAD_W] = g[rows, :HEAD_W]
            o_ref[0, :, base + w:base + w + HEAD_W] = g[rows, HEAD_W:]


def _fftb_kernel(g_ref, f3_ref, wf_ref, o_ref):
    w = GROUP_W
    for kk in range(g_ref.shape[1]):
        blk = g_ref[0, kk]
        gg = jnp.concatenate([blk[:, :w], blk[:, w:]], axis=0).astype(BF16)
        y = jnp.dot(f3_ref[...], gg, preferred_element_type=F32)
        o_ref[0, :, kk * w:(kk + 1) * w] = jnp.dot(y.astype(BF16), wf_ref[...],
                                                   preferred_element_type=F32).astype(o_ref.dtype)


def _fft_big(p_fft, wf_bf, batch, seq):
    w = GROUP_W
    n1 = seq // DFT_N
    t1n = SUBLANES
    c1, s1 = _cos_sin(DFT_N)
    cn, sn = _cos_sin(n1)
    sc = 1.0 / math.sqrt(DFT_N)
    f1 = jnp.asarray(np.concatenate([c1, -s1], axis=0) * sc, BF16)
    wc = jnp.asarray(np.block([[c1, -s1], [s1, c1]]) * sc, BF16)
    f3 = jnp.asarray(np.concatenate([cn, sn], axis=1) / math.sqrt(n1), BF16)
    ang = 2.0 * np.pi * np.outer(np.arange(DFT_N), np.arange(n1)) / seq
    tw = lambda m: jnp.asarray(m.reshape(DFT_N, n1 // t1n, t1n).transpose(1, 0, 2), F32)
    g = pl.pallas_call(
        functools.partial(_ffta_kernel, t1n=t1n),
        grid=(batch, n1 // t1n),
        in_specs=[pl.BlockSpec((1, DFT_N, t1n * w), lambda b, j: (b, 0, j)),
                  pl.BlockSpec(f1.shape, lambda b, j: (0, 0)),
                  pl.BlockSpec((1, DFT_N, t1n), lambda b, j: (j, 0, 0)),
                  pl.BlockSpec((1, DFT_N, t1n), lambda b, j: (j, 0, 0)),
                  pl.BlockSpec(wc.shape, lambda b, j: (0, 0))],
        out_specs=pl.BlockSpec((1, DFT_N, t1n * 2 * w), lambda b, j: (b, 0, j)),
        out_shape=jax.ShapeDtypeStruct((batch, DFT_N, n1 * 2 * w), F32),
        compiler_params=_params("arbitrary", "arbitrary"),
        name="fft_a",
    )(p_fft.reshape(batch, DFT_N, n1 * w), f1, tw(np.cos(ang)), tw(np.sin(ang)), wc)
    y = pl.pallas_call(
        _fftb_kernel,
        grid=(batch, DFT_N // FFT_KB),
        in_specs=[pl.BlockSpec((1, FFT_KB, n1, 2 * w), lambda b, k: (b, k, 0, 0)),
                  pl.BlockSpec(f3.shape, lambda b, k: (0, 0)),
                  pl.BlockSpec((w, w), lambda b, k: (0, 0))],
        out_specs=pl.BlockSpec((1, n1, FFT_KB * w), lambda b, k: (b, 0, k)),
        out_shape=jax.ShapeDtypeStruct((batch, n1, DFT_N * w), BF16),
        compiler_params=_params("arbitrary", "arbitrary"),
        name="fft_b",
    )(g.reshape(batch, DFT_N, n1, 2 * w), f3, wf_bf)
    return y.reshape(batch * seq, w)


def _fftd_kernel(x_ref, cs_ref, fl_ref, wf_ref, o_ref):
    xb = x_ref[...].astype(BF16)
    pa, pb = [], []
    for h in range(HEADS):
        z = jnp.dot(xb[:, h * HEAD_W:(h + 1) * HEAD_W], cs_ref[...], preferred_element_type=F32)
        pa.append(z[:, :HEAD_W])
        pb.append(z[:, HEAD_W:])
    ab = jnp.concatenate([jnp.concatenate(pa, axis=1), jnp.concatenate(pb, axis=1)], axis=0).astype(BF16)
    y = jnp.dot(fl_ref[...], ab, preferred_element_type=F32)
    o_ref[...] = jnp.dot(y.astype(BF16), wf_ref[...], preferred_element_type=F32).astype(o_ref.dtype)


def _fft_dense(p_fft, wf_bf, batch, seq):
    w = GROUP_W
    c1, s1 = _cos_sin(DFT_N)
    cl, sl = _cos_sin(seq)
    cs = jnp.asarray(np.concatenate([c1, s1], axis=1) / math.sqrt(DFT_N), BF16)
    fl = jnp.asarray(np.concatenate([cl, -sl], axis=1) / math.sqrt(seq), BF16)
    return pl.pallas_call(
        _fftd_kernel,
        grid=(batch,),
        in_specs=[pl.BlockSpec((seq, w), lambda b: (b, 0)),
                  pl.BlockSpec(cs.shape, lambda b: (0, 0)),
                  pl.BlockSpec(fl.shape, lambda b: (0, 0)),
                  pl.BlockSpec((w, w), lambda b: (0, 0))],
        out_specs=pl.BlockSpec((seq, w), lambda b: (b, 0)),
        out_shape=jax.ShapeDtypeStruct((batch * seq, w), BF16),
        compiler_params=_params("arbitrary"),
        name="fft_dense",
    )(p_fft, cs, fl, wf_bf)


def _prep_kernel(*refs, use_rope, qscale):
    if use_rope:
        p_ref, cos_ref, sin_ref, qt_ref, k_ref, vt_ref = refs
    else:
        p_ref, qt_ref, k_ref, vt_ref = refs
    t = p_ref.shape[0]
    w = GROUP_W
    if use_rope:
        cos, sin = cos_ref[...], sin_ref[...]
        lane = lax.broadcasted_iota(jnp.int32, (t, HEAD_W), 1)
        first = (lane % (2 * N_FREQ)) < N_FREQ

        def rope(x):
            partner = jnp.where(first, pltpu.roll(x, HEAD_W - N_FREQ, 1), pltpu.roll(x, N_FREQ, 1))
            return x * cos + partner * sin
    else:
        rope = lambda x: x
    for h in range(HEADS):
        q = rope(p_ref[:, h * HEAD_W:(h + 1) * HEAD_W])
        k = rope(p_ref[:, w + h * HEAD_W:w + (h + 1) * HEAD_W])
        v = p_ref[:, 2 * w + h * HEAD_W:2 * w + (h + 1) * HEAD_W]
        qt_ref[0, h] = (q * qscale).T.astype(BF16)
        k_ref[0, h, 0] = k.astype(BF16)
        vt_ref[0, h, 0] = v.T.astype(BF16)


def _attn_prep(p_qkv, cos, sin, batch, seq, t, use_rope):
    nb = seq // t
    in_specs = [pl.BlockSpec((t, 3 * GROUP_W), lambda b, i: (b * nb + i, 0))]
    args = [p_qkv]
    if use_rope:
        in_specs += [pl.BlockSpec((t, HEAD_W), lambda b, i: (i, 0))] * 2
        args += [cos, sin]
    return pl.pallas_call(
        functools.partial(_prep_kernel, use_rope=use_rope, qscale=ATT_DH ** -0.5 * LOG2E),
        grid=(batch, nb),
        in_specs=in_specs,
        out_specs=[pl.BlockSpec((1, HEADS, HEAD_W, t), lambda b, i: (b, 0, 0, i)),
                   pl.BlockSpec((1, HEADS, 1, t, HEAD_W), lambda b, i: (b, 0, i, 0, 0)),
                   pl.BlockSpec((1, HEADS, 1, HEAD_W, t), lambda b, i: (b, 0, i, 0, 0))],
        out_shape=[jax.ShapeDtypeStruct((batch, HEADS, HEAD_W, seq), BF16),
                   jax.ShapeDtypeStruct((batch, HEADS, nb, t, HEAD_W), BF16),
                   jax.ShapeDtypeStruct((batch, HEADS, nb, HEAD_W, t), BF16)],
        compiler_params=_params("arbitrary", "arbitrary"),
        name="attn_prep",
    )(*args)


def _attn_kernel(qt_ref, k_ref, vt_ref, dl_ref, gs_ref, o_ref, s_s, cm_s, acc_s, *, n, lam_init):
    qt = qt_ref[0, 0]
    tq = qt.shape[1]
    row = lax.broadcasted_iota(jnp.int32, qt.shape, 0)
    zero = jnp.zeros_like(qt)
    qs = (jnp.where(row < ATT_DH, qt, zero), jnp.where(row >= ATT_DH, qt, zero))
    acc_s[...] = jnp.zeros_like(acc_s)

    tk = k_ref.shape[3]

    def stage_a(c, buf, width):
        for j in range(2):
            cm = None
            for part in range(width):
                s = jnp.dot(k_ref[0, 0, c + part], qs[j], preferred_element_type=F32)
                s_s[buf, j, part * tk:(part + 1) * tk, :] = s
                top = jnp.max(s, axis=0, keepdims=True)
                cm = top if cm is None else jnp.maximum(cm, top)
            cm_s[buf, j] = cm

    def stage_b(c, buf, width, carry):
        vtb = jnp.concatenate([vt_ref[0, 0, c + part] for part in range(width)], axis=1)
        out = []
        for j in range(2):
            m, l = carry[2 * j], carry[2 * j + 1]
            mn = jnp.maximum(m, cm_s[buf, j])
            alpha = jnp.exp2(m - mn)
            p = jnp.exp2(s_s[buf, j, 0:width * tk, :] - mn)
            l = alpha * l + jnp.sum(p, axis=0, keepdims=True)
            acc_s[j] = alpha * acc_s[j] + jnp.dot(vtb, p.astype(BF16), preferred_element_type=F32)
            out += [mn, l]
        return tuple(out)

    neg = jnp.full((1, tq), -1e30, F32)
    zl = jnp.zeros((1, tq), F32)
    carry = (neg, zl, neg, zl)
    base = n % ATT_WIDE
    for c in range(base):
        stage_a(c, 0, 1)
        carry = stage_b(c, 0, 1, carry)
    nw = n // ATT_WIDE
    first = lambda w: base + ATT_WIDE * w

    def pair(i, carry):
        stage_a(first(2 * i + 1), 1, ATT_WIDE)
        carry = stage_b(first(2 * i), 0, ATT_WIDE, carry)
        stage_a(first(2 * i + 2), 0, ATT_WIDE)
        return stage_b(first(2 * i + 1), 1, ATT_WIDE, carry)

    if nw > 0:
        stage_a(first(0), 0, ATT_WIDE)
        trips = (nw - 1) // 2
        if trips > 0:
            carry = lax.fori_loop(0, trips, pair, carry, unroll=math.gcd(trips, ATT_UNROLL))
        if nw % 2 == 0:
            stage_a(first(nw - 1), 1, ATT_WIDE)
            carry = stage_b(first(nw - 2), 0, ATT_WIDE, carry)
            carry = stage_b(first(nw - 1), 1, ATT_WIDE, carry)
        else:
            carry = stage_b(first(nw - 1), 0, ATT_WIDE, carry)
    dl = dl_ref[...]
    lam = (jnp.exp(jnp.sum(dl[0:1] * dl[1:2], axis=-1, keepdims=True))
           - jnp.exp(jnp.sum(dl[2:3] * dl[3:4], axis=-1, keepdims=True)) + lam_init)
    ot = acc_s[0] / carry[1] - lam * (acc_s[1] / carry[3])
    o_ref[...] = (_rms(ot.T, gs_ref[...]) * (1.0 - lam_init)).astype(o_ref.dtype)


def _attn(qt, k, vt, dl, gs, batch, seq, tq, lam_init):
    nq = seq // tq
    n, tk = k.shape[2], k.shape[3]
    return pl.pallas_call(
        functools.partial(_attn_kernel, n=n, lam_init=lam_init),
        grid=(batch, HEADS, nq),
        in_specs=[pl.BlockSpec((1, 1, HEAD_W, tq), lambda b, h, i: (b, h, 0, i)),
                  pl.BlockSpec((1, 1, n, tk, HEAD_W), lambda b, h, i: (b, h, 0, 0, 0)),
                  pl.BlockSpec((1, 1, n, HEAD_W, tk), lambda b, h, i: (b, h, 0, 0, 0)),
                  pl.BlockSpec(dl.shape, lambda b, h, i: (0, 0)),
                  pl.BlockSpec((1, HEAD_W), lambda b, h, i: (0, 0))],
        out_specs=pl.BlockSpec((tq, HEAD_W), lambda b, h, i: (b * nq + i, h)),
        out_shape=jax.ShapeDtypeStruct((batch * seq, GROUP_W), BF16),
        scratch_shapes=[pltpu.VMEM((2, 2, ATT_WIDE * tk, tq), F32), pltpu.VMEM((2, 2, 1, tq), F32),
                        pltpu.VMEM((2, HEAD_W, tq), F32)],
        compiler_params=_params("arbitrary", "arbitrary", "arbitrary"),
        name="attn",
    )(qt, k, vt, dl, gs)


def _outproj_kernel(yl_ref, yp_ref, yf_ref, ya_ref, w_ref, x_ref, g_ref, ng_ref, o_ref):
    w = GROUP_W
    y = jnp.dot(yl_ref[...], w_ref[0:w, :], preferred_element_type=F32)
    for j, r in enumerate((yp_ref, yf_ref, ya_ref), start=1):
        y = y + jnp.dot(r[...], w_ref[j * w:(j + 1) * w, :], preferred_element_type=F32)
    o_ref[...] = x_ref[...] + g_ref[0] * _rms(y, ng_ref[...])


def _out_proj(ys, w_bf, x2d, modl, ng, tm, tiles_per_batch):
    n, d = x2d.shape
    ytile = pl.BlockSpec((tm, GROUP_W), lambda i: (i, 0))
    return pl.pallas_call(
        _outproj_kernel,
        grid=(n // tm,),
        in_specs=[ytile, ytile, ytile, ytile,
                  pl.BlockSpec(w_bf.shape, lambda i: (0, 0), pipeline_mode=pl.Buffered(1)),
                  pl.BlockSpec((tm, d), lambda i: (i, 0)),
                  _mod_spec(d, 2, tiles_per_batch),
                  pl.BlockSpec((1, d), lambda i: (0, 0))],
        out_specs=pl.BlockSpec((tm, d), lambda i: (i, 0)),
        out_shape=jax.ShapeDtypeStruct((n, d), F32),
        compiler_params=_params("arbitrary"),
        name="out_proj",
    )(*ys, w_bf, x2d, modl, ng)


def _ffn_kernel(x_ref, prev_ref, next_ref, sh_ref, sc_ref, g2_ref, ng2_ref, ng3_ref, wg_ref, wv_ref, cw_ref,
                cb_ref, wd_ref, o_ref, hn_s, gs_s, *, tm, nb, nf):
    i = pl.program_id(0) % nb
    j = pl.program_id(1)
    hr = BF16_ROWS

    @pl.when(j == 0)
    def _():
        norm = lambda v: (_rms(v, ng2_ref[...]) * (1.0 + sc_ref[0]) + sh_ref[0]).astype(BF16)
        hn_s[0:hr, :] = norm(prev_ref[...])
        hn_s[hr:hr + tm, :] = norm(x_ref[...])
        hn_s[hr + tm:2 * hr + tm, :] = norm(next_ref[...])
        o_ref[...] = jnp.zeros_like(o_ref)

    g = jnp.dot(hn_s[...], wg_ref[...], preferred_element_type=F32)
    gs_s[0:hr, :] = jnp.where(i == 0, 0.0, g[0:hr])
    gs_s[hr:hr + tm, :] = g[hr:hr + tm]
    gs_s[hr + tm:2 * hr + tm, :] = jnp.where(i == nb - 1, 0.0, g[hr + tm:2 * hr + tm])
    gc = cb_ref[...] + cw_ref[0:1, :] * gs_s[pl.ds(hr - 1, tm), :]
    for k in range(1, FFN_CONV):
        gc = gc + cw_ref[k:k + 1, :] * gs_s[pl.ds(hr - 1 + k, tm), :]
    v = jnp.dot(hn_s[hr:hr + tm, :], wv_ref[...], preferred_element_type=F32)
    o_ref[...] += jnp.dot((_gelu(gc) * v).astype(BF16), wd_ref[...], preferred_element_type=F32)

    @pl.when(j == nf - 1)
    def _():
        o_ref[...] = x_ref[...] + g2_ref[0] * _rms(o_ref[...], ng3_ref[...])


def _ffn(x2d, modl, ng2, ng3, wup_bf, cw, cb, wdn_bf, seq, tm, tf, tiles_per_batch):
    n, d = x2d.shape
    d_ff = wdn_bf.shape[0]
    nb, nf = seq // tm, d_ff // tf
    hr = BF16_ROWS
    per_t, last = tm // hr, n // hr - 1
    prev = pl.BlockSpec((hr, d), lambda i, j: (jnp.maximum(i * per_t - 1, 0), 0))
    nxt = pl.BlockSpec((hr, d), lambda i, j: (jnp.minimum((i + 1) * per_t, last), 0))
    vec = pl.BlockSpec((1, d), lambda i, j: (0, 0))
    return pl.pallas_call(
        functools.partial(_ffn_kernel, tm=tm, nb=nb, nf=nf),
        grid=(n // tm, nf),
        in_specs=[pl.BlockSpec((tm, d), lambda i, j: (i, 0)), prev, nxt,
                  _mod_spec(d, 3, tiles_per_batch), _mod_spec(d, 4, tiles_per_batch),
                  _mod_spec(d, 5, tiles_per_batch), vec, vec,
                  pl.BlockSpec((d, tf), lambda i, j: (0, j)),
                  pl.BlockSpec((d, tf), lambda i, j: (0, nf + j)),
                  pl.BlockSpec((FFN_CONV, tf), lambda i, j: (0, j)),
                  pl.BlockSpec((1, tf), lambda i, j: (0, j)),
                  pl.BlockSpec((tf, d), lambda i, j: (j, 0))],
        out_specs=pl.BlockSpec((tm, d), lambda i, j: (i, 0)),
        out_shape=jax.ShapeDtypeStruct((n, d), F32),
        scratch_shapes=[pltpu.VMEM((tm + 2 * hr, d), BF16), pltpu.VMEM((tm + 2 * hr, tf), F32)],
        compiler_params=_params("arbitrary", "arbitrary"),
        name="ffn",
    )(x2d, x2d, x2d, modl, modl, modl, ng2, ng3, wup_bf, wup_bf, cw, cb, wdn_bf)


def _rope_tables(seq):
    t = jnp.arange(seq)
    row = (t // GRID_W).astype(F32)
    col = (t % GRID_W).astype(F32)
    inv = ROPE_BASE ** (-jnp.arange(N_FREQ, dtype=F32) / N_FREQ)
    ang_r, ang_c = row[:, None] * inv, col[:, None] * inv
    cos = jnp.concatenate([jnp.cos(ang_r)] * 2 + [jnp.cos(ang_c)] * 2, axis=1)
    sin = jnp.concatenate([-jnp.sin(ang_r), jnp.sin(ang_r), -jnp.sin(ang_c), jnp.sin(ang_c)], axis=1)
    return jnp.tile(cos, (1, 2)), jnp.tile(sin, (1, 2))


def _gate_weights(wa, ba, wi, bi):
    eye = jnp.eye(LRU_BLOCKS, dtype=wa.dtype)
    dense = lambda m: jnp.einsum('nde,nm->ndme', m, eye).reshape(GROUP_W, GROUP_W)
    return (jnp.concatenate([dense(wa), dense(wi)], axis=1).astype(BF16),
            jnp.concatenate([ba, bi])[None, :])


def _row_tile(seq):
    return min(seq, 512)


def _mixers(p, layer, batch, seq, h0f, h0b, rope, prefix, lam_init, want_y):
    p_lru, p_pool, p_fft, p_qkv = p
    t = _row_tile(seq)
    hb, st_b = _lru(p_lru, None, layer['cw'], layer['cb'], layer['wg'][1], layer['bg'][1], layer['lam'][1],
                    h0b, batch, seq, t, True)
    y_lru, st_f = _lru(p_lru, hb, layer['cw'], layer['cb'], layer['wg'][0], layer['bg'][0], layer['lam'][0],
                       h0f, batch, seq, t, False)
    use_rope = rope is not None
    qt, k, vt = _attn_prep(p_qkv, *(rope if use_rope else (None, None)), batch, seq, ATT_TK, use_rope)
    ys = None
    if want_y:
        y_pool = _pool(p_pool, layer['wp'], layer['sp'], batch, seq, t)
        if seq % (DFT_N * SUBLANES) == 0:
            y_fft = _fft_big(p_fft, layer['wf'], batch, seq)
        else:
            y_fft = _fft_dense(p_fft, layer['wf'], batch, seq)
        if prefix is not None:
            k_all, vt_all = (jnp.concatenate([pre, cur], axis=2) for pre, cur in zip(prefix, (k, vt)))
        else:
            k_all, vt_all = k, vt
        y_att = _attn(qt, k_all, vt_all, layer['dl'], layer['gs'], batch, seq, t, lam_init)
        ys = (y_lru, y_pool, y_fft, y_att)
    return ys, (st_f, st_b), (k, vt)


def kernel(x, c, ctx, c_ctx, ada_w, ada_b, norm_g, w_in, lru_conv_w, lru_conv_b, lru_wa, lru_ba, lru_wi, lru_bi,
           lru_lam, pool_w, pool_scale, fourier_w, diff_lam, diff_subln_g, w_out, ffn_w_up, ffn_conv_w,
           ffn_conv_b, ffn_w_down):
    batch, seq, d = x.shape
    ctx_len = ctx.shape[1]
    depth = ada_w.shape[0]
    d_ff = ffn_w_down.shape[1]
    tm, tmc = _row_tile(seq), _row_tile(ctx_len)
    tpb = seq // tm
    tf = 512

    c8 = jnp.concatenate([c.astype(F32), c_ctx.astype(F32)[None, :],
                          jnp.zeros((SUBLANES - batch - 1, d), F32)], axis=0)
    mod = _ada(c8, ada_w, ada_b).reshape(depth, SUBLANES * N_MOD, 1, d)
    rope = _rope_tables(seq)
    zeros_state = jnp.zeros((batch, 1, GROUP_W), F32)

    x2 = x.reshape(batch * seq, d)
    xc = ctx.reshape(batch * ctx_len, d)
    for l in range(depth):
        last = l == depth - 1
        lam_init = 0.8 - 0.6 * math.exp(-0.3 * l)
        gates = [_gate_weights(lru_wa[l, dr], lru_ba[l, dr], lru_wi[l, dr], lru_bi[l, dr]) for dr in range(2)]
        layer = dict(cw=lru_conv_w[l], cb=lru_conv_b[l][None, :],
                     wg=[g[0] for g in gates], bg=[g[1] for g in gates],
                     lam=[lru_lam[l, dr][None, :] for dr in range(2)],
                     wp=pool_w[l].astype(BF16), sp=pool_scale[l][None, :], wf=fourier_w[l].astype(BF16),
                     dl=diff_lam[l], gs=diff_subln_g[l][None, :])
        w_in_bf, w_out_bf = w_in[l].astype(BF16), w_out[l].astype(BF16)
        w_up_bf, w_dn_bf = ffn_w_up[l].astype(BF16), ffn_w_down[l].astype(BF16)
        ng = [norm_g[l, k][None, :] for k in range(4)]
        modl = mod[l]
        ffn = functools.partial(_ffn, ng2=ng[2], ng3=ng[3], wup_bf=w_up_bf, cw=ffn_conv_w[l],
                                cb=ffn_conv_b[l][None, :], wdn_bf=w_dn_bf, tf=tf)

        pc = _in_proj(xc, modl, ng[0], w_in_bf, tmc, None)
        ysc, (stf, stb), ctx_kv = _mixers(pc, layer, batch, ctx_len, zeros_state, zeros_state, None, None,
                                          lam_init, not last)
        pz = _in_proj(x2, modl, ng[0], w_in_bf, tm, tpb)
        ys, _, _ = _mixers(pz, layer, batch, seq, stf[:, :1], stb[:, :1], rope, ctx_kv, lam_init, True)
        x2 = _out_proj(ys, w_out_bf, x2, modl, ng[1], tm, tpb)
        x2 = ffn(x2, modl, seq=seq, tm=tm, tiles_per_batch=tpb)
        if not last:
            xc = _out_proj(ysc, w_out_bf, xc, modl, ng[1], tmc, None)
            xc = ffn(xc, modl, seq=ctx_len, tm=tmc, tiles_per_batch=None)
    return x2.reshape(batch, seq, d)
```

```python
import functools
import math

import numpy as np
import jax
import jax.numpy as jnp
from jax import lax
from jax.experimental import pallas as pl
from jax.experimental.pallas import tpu as pltpu

F32 = jnp.float32
BF16 = jnp.bfloat16

EPS = 1e-6
N_MOD = 6
GROUP_W = 512
LRU_BLOCKS = 8
LRU_CONV = 4
LRU_C = 8.0
SQRT_GUARD = 1e-37
POOL_HALF = (1, 2, 4, 8)
HEADS = 4
HEAD_W = 128
ATT_DH = 64
ATT_TK = 256
ATT_WIDE = 2
ATT_UNROLL = 3
N_FREQ = ATT_DH // 4
ROPE_BASE = 10000.0
GRID_W = 64
FFN_CONV = 3
SUBLANES = 8
BF16_ROWS = 16
VT_ROWS = HEAD_W + BF16_ROWS
DFT_N = 128
FFT_KB = 4
V7X_VMEM_LIMIT = 56 * 1024 * 1024
LOG2E = 1.4426950408889634


def _params(*sem):
    return pltpu.CompilerParams(dimension_semantics=sem, vmem_limit_bytes=V7X_VMEM_LIMIT)


def _rms(xf, g):
    return xf * lax.rsqrt(jnp.mean(xf * xf, axis=-1, keepdims=True) + EPS) * g


def _gelu(x):
    return x * (0.5 * (1.0 + jnp.tanh(0.7978845608028654 * (x + 0.044715 * (x * x * x)))))


def _sigmoid(x):
    return 1.0 / (1.0 + jnp.exp(-x))


def _ada_kernel(c_ref, w_ref, b_ref, o_ref):
    c = c_ref[...]
    s = (c * _sigmoid(c)).astype(BF16)
    o_ref[0] = jnp.dot(s, w_ref[0].astype(BF16), preferred_element_type=F32) + b_ref[0]


def _ada(c8, ada_w, ada_b):
    depth, d, nm = ada_w.shape
    tn = 1024
    return pl.pallas_call(
        _ada_kernel,
        grid=(depth, nm // tn),
        in_specs=[pl.BlockSpec((SUBLANES, d), lambda l, j: (0, 0)),
                  pl.BlockSpec((1, d, tn), lambda l, j: (l, 0, j)),
                  pl.BlockSpec((1, 1, tn), lambda l, j: (l, 0, j))],
        out_specs=pl.BlockSpec((1, SUBLANES, tn), lambda l, j: (l, 0, j)),
        out_shape=jax.ShapeDtypeStruct((depth, SUBLANES, nm), F32),
        compiler_params=_params("arbitrary", "arbitrary"),
        name="ada",
    )(c8, ada_w, ada_b.reshape(depth, 1, nm))


def _mod_spec(d, which, tiles_per_batch):
    if tiles_per_batch is None:
        return pl.BlockSpec((1, 1, d), lambda i, *_: (2 * N_MOD + which, 0, 0))
    return pl.BlockSpec((1, 1, d), lambda i, *_: ((i // tiles_per_batch) * N_MOD + which, 0, 0))


def _inproj_kernel(*refs, use_rope, qscale):
    if use_rope:
        (x_ref, sh_ref, sc_ref, g_ref, w_ref, cos_ref, sin_ref,
         lru_ref, pool_ref, fft_ref, qt_ref, k_ref, vt_ref) = refs
    else:
        x_ref, sh_ref, sc_ref, g_ref, w_ref, lru_ref, pool_ref, fft_ref, qt_ref, k_ref, vt_ref = refs
    w = GROUP_W
    h = _rms(x_ref[...], g_ref[...]) * (1.0 + sc_ref[0]) + sh_ref[0]
    p = jnp.dot(h.astype(BF16), w_ref[...], preferred_element_type=F32)
    lru_ref[...] = p[:, :2 * w]
    pool_ref[...] = p[:, 2 * w:3 * w]
    fft_ref[...] = p[:, 3 * w:4 * w]
    tm = p.shape[0]
    if use_rope:
        cos, sin = cos_ref[...], sin_ref[...]
        lane = lax.broadcasted_iota(jnp.int32, (tm, HEAD_W), 1)
        first = (lane % (2 * N_FREQ)) < N_FREQ

        def rope(v):
            partner = jnp.where(first, pltpu.roll(v, HEAD_W - N_FREQ, 1), pltpu.roll(v, N_FREQ, 1))
            return v * cos + partner * sin
    else:
        rope = lambda v: v
    ones = jnp.ones((VT_ROWS - HEAD_W, ATT_TK), BF16)
    for hd in range(HEADS):
        cols = lambda grp: slice((4 + grp) * w + hd * HEAD_W, (4 + grp) * w + (hd + 1) * HEAD_W)
        qt_ref[0, hd] = (rope(p[:, cols(0)]) * qscale).T.astype(BF16)
        kb = rope(p[:, cols(1)]).astype(BF16)
        vtb = p[:, cols(2)].T.astype(BF16)
        for ck in range(tm // ATT_TK):
            k_ref[0, hd, ck] = kb[ck * ATT_TK:(ck + 1) * ATT_TK, :]
            vt_ref[0, hd, ck, 0:HEAD_W, :] = vtb[:, ck * ATT_TK:(ck + 1) * ATT_TK]
            vt_ref[0, hd, ck, HEAD_W:VT_ROWS, :] = ones


def _in_proj(x2d, modl, ng, w_bf, rope, batch, seq, tm, is_ctx):
    n, d = x2d.shape
    d_in = w_bf.shape[1]
    w = GROUP_W
    tpb = seq // tm
    nck = tm // ATT_TK
    use_rope = rope is not None
    in_specs = [pl.BlockSpec((tm, d), lambda i: (i, 0)),
                _mod_spec(d, 0, None if is_ctx else tpb),
                _mod_spec(d, 1, None if is_ctx else tpb),
                pl.BlockSpec((1, d), lambda i: (0, 0)),
                pl.BlockSpec((d, d_in), lambda i: (0, 0), pipeline_mode=pl.Buffered(1))]
    args = [x2d, modl, modl, ng, w_bf]
    if use_rope:
        in_specs += [pl.BlockSpec((tm, HEAD_W), lambda i: (i % tpb, 0))] * 2
        args += list(rope)
    row = lambda wd: pl.BlockSpec((tm, wd), lambda i: (i, 0))
    return pl.pallas_call(
        functools.partial(_inproj_kernel, use_rope=use_rope, qscale=ATT_DH ** -0.5 * LOG2E),
        grid=(n // tm,),
        in_specs=in_specs,
        out_specs=[row(2 * w), row(w), row(w),
                   pl.BlockSpec((1, HEADS, HEAD_W, tm), lambda i: (i // tpb, 0, 0, i % tpb)),
                   pl.BlockSpec((1, HEADS, nck, ATT_TK, HEAD_W), lambda i: (i // tpb, 0, i % tpb, 0, 0)),
                   pl.BlockSpec((1, HEADS, nck, VT_ROWS, ATT_TK), lambda i: (i // tpb, 0, i % tpb, 0, 0))],
        out_shape=[jax.ShapeDtypeStruct((n, 2 * w), F32), jax.ShapeDtypeStruct((n, w), F32),
                   jax.ShapeDtypeStruct((n, w), F32),
                   jax.ShapeDtypeStruct((batch, HEADS, HEAD_W, seq), BF16),
                   jax.ShapeDtypeStruct((batch, HEADS, seq // ATT_TK, ATT_TK, HEAD_W), BF16),
                   jax.ShapeDtypeStruct((batch, HEADS, seq // ATT_TK, VT_ROWS, ATT_TK), BF16)],
        compiler_params=_params("arbitrary"),
        name="in_proj",
    )(*args)


def _halo_specs(width, col, n_rows, seq, t, blk_of):
    per_b, per_t, last = seq // SUBLANES, t // SUBLANES, n_rows // SUBLANES - 1
    prev = pl.BlockSpec((SUBLANES, width),
                        lambda b, i, *_: (jnp.maximum(b * per_b + blk_of(i) * per_t - 1, 0), col))
    nxt = pl.BlockSpec((SUBLANES, width),
                       lambda b, i, *_: (jnp.minimum(b * per_b + (blk_of(i) + 1) * per_t, last), col))
    return prev, nxt


def _fill_ext(ext_s, x, prev, nxt, blk, nb, t):
    ext_s[0:SUBLANES, :] = jnp.where(blk == 0, 0.0, prev)
    ext_s[SUBLANES:SUBLANES + t, :] = x
    ext_s[SUBLANES + t:2 * SUBLANES + t, :] = jnp.where(blk == nb - 1, 0.0, nxt)


def _lru_kernel(*refs, t, nb, reverse, emit_y):
    if emit_y:
        (ux_ref, prev_ref, next_ref, gate_ref, hb_ref, cw_ref, cb_ref, wg_ref, bg_ref, lam_ref, h0_ref,
         out_ref, st_ref, ext_s, car_s) = refs
    else:
        (ux_ref, prev_ref, next_ref, cw_ref, cb_ref, wg_ref, bg_ref, lam_ref, h0_ref,
         out_ref, st_ref, ext_s, car_s) = refs
    i = pl.program_id(1)
    blk = (nb - 1 - i) if reverse else i
    w = GROUP_W
    ngrp = w // HEAD_W

    @pl.when(i == 0)
    def _():
        car_s[...] = h0_ref[0]

    _fill_ext(ext_s, ux_ref[...], prev_ref[...], next_ref[...], blk, nb, t)
    u = cb_ref[...] + cw_ref[0:1, :] * ext_s[pl.ds(SUBLANES - 2, t), :]
    for k in range(1, LRU_CONV):
        u = u + cw_ref[k:k + 1, :] * ext_s[pl.ds(SUBLANES - 2 + k, t), :]

    g = jnp.dot(u.astype(BF16), wg_ref[...], preferred_element_type=F32) + bg_ref[...]
    r = _sigmoid(g[:, :w])
    ig = _sigmoid(g[:, w:])
    nl = -lam_ref[...]
    softplus = jnp.maximum(nl, 0.0) + jnp.log1p(jnp.exp(-jnp.abs(nl)))
    log_a = (-LRU_C) * r * softplus
    a = jnp.exp(log_a)
    one_m = 1.0 - a * a
    bb = (one_m * lax.rsqrt(jnp.maximum(one_m, SQRT_GUARD))) * (ig * u)

    ng = t // SUBLANES
    rowid = lax.broadcasted_iota(jnp.int32, (1, SUBLANES, HEAD_W), 1)
    c_all = car_s[...]
    c_new, h_cols = [], []
    for cg in range(ngrp):
        av = a[:, cg * HEAD_W:(cg + 1) * HEAD_W].reshape(ng, SUBLANES, HEAD_W)
        bv = bb[:, cg * HEAD_W:(cg + 1) * HEAD_W].reshape(ng, SUBLANES, HEAD_W)
        for dist in (1, 2, 4):
            shift = SUBLANES - dist if reverse else dist
            valid = (rowid < SUBLANES - dist) if reverse else (rowid >= dist)
            b_far = jnp.where(valid, pltpu.roll(bv, shift, 1), 0.0)
            a_far = jnp.where(valid, pltpu.roll(av, shift, 1), 1.0)
            bv = bv + av * b_far
            av = av * a_far
        c = c_all[:, cg * HEAD_W:(cg + 1) * HEAD_W]
        hs = [None] * ng
        for g in (range(ng - 1, -1, -1) if reverse else range(ng)):
            hs[g] = bv[g] + av[g] * c
            c = hs[g][0:1, :] if reverse else hs[g][SUBLANES - 1:SUBLANES, :]
        c_new.append(c)
        h_cols.append(jnp.concatenate(hs, axis=0))
    c = jnp.concatenate(c_new, axis=1)
    car_s[...] = c
    st_ref[0] = jnp.broadcast_to(c, (SUBLANES, w))
    hfull = jnp.concatenate(h_cols, axis=1)
    if emit_y:
        out_ref[...] = (_gelu(gate_ref[...]) * (hfull + hb_ref[...])).astype(out_ref.dtype)
    else:
        out_ref[...] = hfull


def _lru(p_lru, hb, cw, cb, wg, bg, lam, h0, batch, seq, t, reverse):
    n = p_lru.shape[0]
    w = GROUP_W
    nb = seq // t
    emit_y = not reverse
    blk_of = (lambda i: nb - 1 - i) if reverse else (lambda i: i)
    main = lambda col: pl.BlockSpec((t, w), lambda b, i: (b * nb + blk_of(i), col))
    prev, nxt = _halo_specs(w, 0, n, seq, t, blk_of)
    const = lambda shape: pl.BlockSpec(shape, lambda b, i: (0,) * len(shape))
    in_specs = [main(0), prev, nxt]
    args = [p_lru, p_lru, p_lru]
    if emit_y:
        in_specs += [main(1), main(0)]
        args += [p_lru, hb]
    in_specs += [const((LRU_CONV, w)), const((1, w)), const((w, 2 * w)), const((1, 2 * w)), const((1, w)),
                 pl.BlockSpec((1, 1, w), lambda b, i: (b, 0, 0))]
    args += [cw, cb, wg, bg, lam, h0]
    scratch = [pltpu.VMEM((t + 2 * SUBLANES, w), F32), pltpu.VMEM((1, w), F32)]
    return pl.pallas_call(
        functools.partial(_lru_kernel, t=t, nb=nb, reverse=reverse, emit_y=emit_y),
        grid=(batch, nb),
        in_specs=in_specs,
        out_specs=[pl.BlockSpec((t, w), lambda b, i: (b * nb + blk_of(i), 0)),
                   pl.BlockSpec((1, SUBLANES, w), lambda b, i: (b, 0, 0))],
        out_shape=[jax.ShapeDtypeStruct((n, w), BF16 if emit_y else F32),
                   jax.ShapeDtypeStruct((batch, SUBLANES, w), F32)],
        scratch_shapes=scratch,
        compiler_params=_params("arbitrary", "arbitrary"),
        name="lru_fwd" if emit_y else "lru_bwd",
    )(*args)


def _pool_kernel(x_ref, prev_ref, next_ref, wp_ref, sp_ref, o_ref, ext_s, *, t, nb, seq):
    i = pl.program_id(1)
    x = x_ref[...]
    _fill_ext(ext_s, x, prev_ref[...], next_ref[...], i, nb, t)
    pos = i * t + lax.broadcasted_iota(jnp.int32, (t, HEAD_W), 0)
    for g, hw in enumerate(POOL_HALF):
        cols = slice(g * HEAD_W, (g + 1) * HEAD_W)
        s = ext_s[pl.ds(SUBLANES - hw, t), cols]
        for dlt in range(-hw + 1, hw):
            s = s + ext_s[pl.ds(SUBLANES + dlt, t), cols]
        cnt = (jnp.minimum(pos + hw - 1, seq - 1) - jnp.maximum(pos - hw, 0) + 1).astype(F32)
        dif = s / cnt - x[:, cols]
        y = jnp.dot(dif.astype(BF16), wp_ref[g], preferred_element_type=F32) * sp_ref[:, cols]
        o_ref[:, cols] = y.astype(o_ref.dtype)


def _pool(p_pool, wp_bf, sp, batch, seq, t):
    n, w = p_pool.shape
    nb = seq // t
    prev, nxt = _halo_specs(w, 0, n, seq, t, lambda i: i)
    return pl.pallas_call(
        functools.partial(_pool_kernel, t=t, nb=nb, seq=seq),
        grid=(batch, nb),
        in_specs=[pl.BlockSpec((t, w), lambda b, i: (b * nb + i, 0)), prev, nxt,
                  pl.BlockSpec(wp_bf.shape, lambda b, i: (0, 0, 0)),
                  pl.BlockSpec((1, w), lambda b, i: (0, 0))],
        out_specs=pl.BlockSpec((t, w), lambda b, i: (b * nb + i, 0)),
        out_shape=jax.ShapeDtypeStruct((n, w), BF16),
        scratch_shapes=[pltpu.VMEM((t + 2 * SUBLANES, w), F32)],
        compiler_params=_params("arbitrary", "arbitrary"),
        name="pool",
    )(p_pool, p_pool, p_pool, wp_bf, sp)


def _cos_sin(n):
    ang = 2.0 * np.pi * np.outer(np.arange(n), np.arange(n)) / n
    return np.cos(ang), np.sin(ang)


def _ffta_kernel(x_ref, f1_ref, twc_ref, tws_ref, wc_ref, o_ref, *, t1n):
    w = GROUP_W
    hh = jnp.dot(f1_ref[...], x_ref[0].astype(BF16), preferred_element_type=F32)
    hr, hi = hh[:DFT_N], hh[DFT_N:]
    twc, tws = twc_ref[0], tws_ref[0]
    parts_r, parts_i = [], []
    for t1 in range(t1n):
        cw, sw = twc[:, t1:t1 + 1], tws[:, t1:t1 + 1]
        ar, ai = hr[:, t1 * w:(t1 + 1) * w], hi[:, t1 * w:(t1 + 1) * w]
        br = ar * cw + ai * sw
        bi = ai * cw - ar * sw
        for h in range(HEADS):
            parts_r.append(br[:, h * HEAD_W:(h + 1) * HEAD_W])
            parts_i.append(bi[:, h * HEAD_W:(h + 1) * HEAD_W])
    ab = jnp.concatenate([jnp.concatenate(parts_r, axis=0), jnp.concatenate(parts_i, axis=0)], axis=1)
    g = jnp.dot(ab.astype(BF16), wc_ref[...], preferred_element_type=F32)
    for t1 in range(t1n):
        for h in range(HEADS):
            rows = slice((t1 * HEADS + h) * DFT_N, (t1 * HEADS + h + 1) * DFT_N)
            base = t1 * 2 * w + h * HEAD_W
            o_ref[0, :, base:base + HEAD_W] = g[rows, :HEAD_W]
            o_ref[0, :, base + w:base + w + HEAD_W] = g[rows, HEAD_W:]


def _fftb_kernel(g_ref, f3_ref, wf_ref, o_ref):
    w = GROUP_W
    for kk in range(g_ref.shape[1]):
        blk = g_ref[0, kk]
        gg = jnp.concatenate([blk[:, :w], blk[:, w:]], axis=0).astype(BF16)
        y = jnp.dot(f3_ref[...], gg, preferred_element_type=F32)
        o_ref[0, :, kk * w:(kk + 1) * w] = jnp.dot(y.astype(BF16), wf_ref[...],
                                                   preferred_element_type=F32).astype(o_ref.dtype)


def _fft_big(p_fft, wf_bf, batch, seq):
    w = GROUP_W
    n1 = seq // DFT_N
    t1n = SUBLANES
    c1, s1 = _cos_sin(DFT_N)
    cn, sn = _cos_sin(n1)
    sc = 1.0 / math.sqrt(DFT_N)
    f1 = jnp.asarray(np.concatenate([c1, -s1], axis=0) * sc, BF16)
    wc = jnp.asarray(np.block([[c1, -s1], [s1, c1]]) * sc, BF16)
    f3 = jnp.asarray(np.concatenate([cn, sn], axis=1) / math.sqrt(n1), BF16)
    ang = 2.0 * np.pi * np.outer(np.arange(DFT_N), np.arange(n1)) / seq
    tw = lambda m: jnp.asarray(m.reshape(DFT_N, n1 // t1n, t1n).transpose(1, 0, 2), F32)
    g = pl.pallas_call(
        functools.partial(_ffta_kernel, t1n=t1n),
        grid=(batch, n1 // t1n),
        in_specs=[pl.BlockSpec((1, DFT_N, t1n * w), lambda b, j: (b, 0, j)),
                  pl.BlockSpec(f1.shape, lambda b, j: (0, 0)),
                  pl.BlockSpec((1, DFT_N, t1n), lambda b, j: (j, 0, 0)),
                  pl.BlockSpec((1, DFT_N, t1n), lambda b, j: (j, 0, 0)),
                  pl.BlockSpec(wc.shape, lambda b, j: (0, 0))],
        out_specs=pl.BlockSpec((1, DFT_N, t1n * 2 * w), lambda b, j: (b, 0, j)),
        out_shape=jax.ShapeDtypeStruct((batch, DFT_N, n1 * 2 * w), F32),
        compiler_params=_params("arbitrary", "arbitrary"),
        name="fft_a",
    )(p_fft.reshape(batch, DFT_N, n1 * w), f1, tw(np.cos(ang)), tw(np.sin(ang)), wc)
    y = pl.pallas_call(
        _fftb_kernel,
        grid=(batch, DFT_N // FFT_KB),
        in_specs=[pl.BlockSpec((1, FFT_KB, n1, 2 * w), lambda b, k: (b, k, 0, 0)),
                  pl.BlockSpec(f3.shape, lambda b, k: (0, 0)),
                  pl.BlockSpec((w, w), lambda b, k: (0, 0))],
        out_specs=pl.BlockSpec((1, n1, FFT_KB * w), lambda b, k: (b, 0, k)),
        out_shape=jax.ShapeDtypeStruct((batch, n1, DFT_N * w), BF16),
        compiler_params=_params("arbitrary", "arbitrary"),
        name="fft_b",
    )(g.reshape(batch, DFT_N, n1, 2 * w), f3, wf_bf)
    return y.reshape(batch * seq, w)


def _fftd_kernel(x_ref, cs_ref, fl_ref, wf_ref, o_ref):
    xb = x_ref[...].astype(BF16)
    pa, pb = [], []
    for h in range(HEADS):
        z = jnp.dot(xb[:, h * HEAD_W:(h + 1) * HEAD_W], cs_ref[...], preferred_element_type=F32)
        pa.append(z[:, :HEAD_W])
        pb.append(z[:, HEAD_W:])
    ab = jnp.concatenate([jnp.concatenate(pa, axis=1), jnp.concatenate(pb, axis=1)], axis=0).astype(BF16)
    y = jnp.dot(fl_ref[...], ab, preferred_element_type=F32)
    o_ref[...] = jnp.dot(y.astype(BF16), wf_ref[...], preferred_element_type=F32).astype(o_ref.dtype)


def _fft_dense(p_fft, wf_bf, batch, seq):
    w = GROUP_W
    c1, s1 = _cos_sin(DFT_N)
    cl, sl = _cos_sin(seq)
    cs = jnp.asarray(np.concatenate([c1, s1], axis=1) / math.sqrt(DFT_N), BF16)
    fl = jnp.asarray(np.concatenate([cl, -sl], axis=1) / math.sqrt(seq), BF16)
    return pl.pallas_call(
        _fftd_kernel,
        grid=(batch,),
        in_specs=[pl.BlockSpec((seq, w), lambda b: (b, 0)),
                  pl.BlockSpec(cs.shape, lambda b: (0, 0)),
                  pl.BlockSpec(fl.shape, lambda b: (0, 0)),
                  pl.BlockSpec((w, w), lambda b: (0, 0))],
        out_specs=pl.BlockSpec((seq, w), lambda b: (b, 0)),
        out_shape=jax.ShapeDtypeStruct((batch * seq, w), BF16),
        compiler_params=_params("arbitrary"),
        name="fft_dense",
    )(p_fft, cs, fl, wf_bf)


def _attn_kernel(qt_ref, k_ref, vt_ref, dl_ref, gs_ref, o_ref, s_s, cm_s, acc_s, *, n, lam_init):
    qt = qt_ref[0, 0]
    tq = qt.shape[1]
    row = lax.broadcasted_iota(jnp.int32, qt.shape, 0)
    zero = jnp.zeros_like(qt)
    qs = (jnp.where(row < ATT_DH, qt, zero), jnp.where(row >= ATT_DH, qt, zero))
    acc_s[...] = jnp.zeros_like(acc_s)

    tk = k_ref.shape[3]

    def stage_a(c, buf, width):
        for j in range(2):
            cm = None
            for part in range(width):
                s = jnp.dot(k_ref[0, 0, c + part], qs[j], preferred_element_type=F32)
                s_s[buf, j, part * tk:(part + 1) * tk, :] = s
                top = jnp.max(s, axis=0, keepdims=True)
                cm = top if cm is None else jnp.maximum(cm, top)
            cm_s[buf, j] = cm

    def stage_b(c, buf, width, ms):
        vtb = jnp.concatenate([vt_ref[0, 0, c + part] for part in range(width)], axis=1)
        out = []
        for j in range(2):
            mn = jnp.maximum(ms[j], cm_s[buf, j])
            alpha = jnp.exp2(ms[j] - mn)
            p = jnp.exp2(s_s[buf, j, 0:width * tk, :] - mn)
            acc_s[j] = alpha * acc_s[j] + jnp.dot(vtb, p.astype(BF16), preferred_element_type=F32)
            out.append(mn)
        return tuple(out)

    neg = jnp.full((1, tq), -1e30, F32)
    base = n % ATT_WIDE
    nw = n // ATT_WIDE
    first = lambda w: base + ATT_WIDE * w
    order = [(first(w), ATT_WIDE) for w in range(nw)] + [(c, 1) for c in range(base)]

    def pair(i, ms):
        stage_a(first(2 * i + 1), 1, ATT_WIDE)
        ms = stage_b(first(2 * i), 0, ATT_WIDE, ms)
        stage_a(first(2 * i + 2), 0, ATT_WIDE)
        return stage_b(first(2 * i + 1), 1, ATT_WIDE, ms)

    stage_a(order[0][0], 0, order[0][1])
    trips = max(nw - 1, 0) // 2
    ms = (neg, neg)
    if trips > 0:
        ms = lax.fori_loop(0, trips, pair, ms, unroll=math.gcd(trips, ATT_UNROLL))
    tail = order[2 * trips:]
    for idx, (c, width) in enumerate(tail):
        step = 2 * trips + idx
        if idx + 1 < len(tail):
            stage_a(tail[idx + 1][0], (step + 1) % 2, tail[idx + 1][1])
        ms = stage_b(c, step % 2, width, ms)
    dl = dl_ref[...]
    lam = (jnp.exp(jnp.sum(dl[0:1] * dl[1:2], axis=-1, keepdims=True))
           - jnp.exp(jnp.sum(dl[2:3] * dl[3:4], axis=-1, keepdims=True)) + lam_init)
    o0 = acc_s[0, 0:HEAD_W, :] / acc_s[0, HEAD_W:HEAD_W + 1, :]
    o1 = acc_s[1, 0:HEAD_W, :] / acc_s[1, HEAD_W:HEAD_W + 1, :]
    ot = o0 - lam * o1
    o_ref[...] = (_rms(ot.T, gs_ref[...]) * (1.0 - lam_init)).astype(o_ref.dtype)


def _attn(qt, k, vt, dl, gs, batch, seq, tq, lam_init):
    nq = seq // tq
    n, tk = k.shape[2], k.shape[3]
    return pl.pallas_call(
        functools.partial(_attn_kernel, n=n, lam_init=lam_init),
        grid=(batch, HEADS, nq),
        in_specs=[pl.BlockSpec((1, 1, HEAD_W, tq), lambda b, h, i: (b, h, 0, i)),
                  pl.BlockSpec((1, 1, n, tk, HEAD_W), lambda b, h, i: (b, h, 0, 0, 0)),
                  pl.BlockSpec((1, 1, n, VT_ROWS, tk), lambda b, h, i: (b, h, 0, 0, 0)),
                  pl.BlockSpec(dl.shape, lambda b, h, i: (0, 0)),
                  pl.BlockSpec((1, HEAD_W), lambda b, h, i: (0, 0))],
        out_specs=pl.BlockSpec((tq, HEAD_W), lambda b, h, i: (b * nq + i, h)),
        out_shape=jax.ShapeDtypeStruct((batch * seq, GROUP_W), BF16),
        scratch_shapes=[pltpu.VMEM((2, 2, ATT_WIDE * tk, tq), F32), pltpu.VMEM((2, 2, 1, tq), F32),
                        pltpu.VMEM((2, VT_ROWS, tq), F32)],
        compiler_params=_params("arbitrary", "arbitrary", "arbitrary"),
        name="attn",
    )(qt, k, vt, dl, gs)


def _outproj_kernel(yl_ref, yp_ref, yf_ref, ya_ref, w_ref, x_ref, g_ref, ng_ref, o_ref):
    w = GROUP_W
    y = jnp.dot(yl_ref[...], w_ref[0:w, :], preferred_element_type=F32)
    for j, r in enumerate((yp_ref, yf_ref, ya_ref), start=1):
        y = y + jnp.dot(r[...], w_ref[j * w:(j + 1) * w, :], preferred_element_type=F32)
    o_ref[...] = x_ref[...] + g_ref[0] * _rms(y, ng_ref[...])


def _out_proj(ys, w_bf, x2d, modl, ng, tm, tiles_per_batch):
    n, d = x2d.shape
    ytile = pl.BlockSpec((tm, GROUP_W), lambda i: (i, 0))
    return pl.pallas_call(
        _outproj_kernel,
        grid=(n // tm,),
        in_specs=[ytile, ytile, ytile, ytile,
                  pl.BlockSpec(w_bf.shape, lambda i: (0, 0), pipeline_mode=pl.Buffered(1)),
                  pl.BlockSpec((tm, d), lambda i: (i, 0)),
                  _mod_spec(d, 2, tiles_per_batch),
                  pl.BlockSpec((1, d), lambda i: (0, 0))],
        out_specs=pl.BlockSpec((tm, d), lambda i: (i, 0)),
        out_shape=jax.ShapeDtypeStruct((n, d), F32),
        compiler_params=_params("arbitrary"),
        name="out_proj",
    )(*ys, w_bf, x2d, modl, ng)


def _ffn_kernel(x_ref, prev_ref, next_ref, sh_ref, sc_ref, g2_ref, ng2_ref, ng3_ref, wg_ref, wv_ref, cw_ref,
                cb_ref, wd_ref, o_ref, hn_s, gs_s, *, tm, nb, nf):
    i = pl.program_id(0) % nb
    j = pl.program_id(1)
    hr = BF16_ROWS

    @pl.when(j == 0)
    def _():
        norm = lambda v: (_rms(v, ng2_ref[...]) * (1.0 + sc_ref[0]) + sh_ref[0]).astype(BF16)
        hn_s[0:hr, :] = norm(prev_ref[...])
        hn_s[hr:hr + tm, :] = norm(x_ref[...])
        hn_s[hr + tm:2 * hr + tm, :] = norm(next_ref[...])
        o_ref[...] = jnp.zeros_like(o_ref)

    g = jnp.dot(hn_s[...], wg_ref[...], preferred_element_type=F32)
    gs_s[0:hr, :] = jnp.where(i == 0, 0.0, g[0:hr])
    gs_s[hr:hr + tm, :] = g[hr:hr + tm]
    gs_s[hr + tm:2 * hr + tm, :] = jnp.where(i == nb - 1, 0.0, g[hr + tm:2 * hr + tm])
    gc = cb_ref[...] + cw_ref[0:1, :] * gs_s[pl.ds(hr - 1, tm), :]
    for k in range(1, FFN_CONV):
        gc = gc + cw_ref[k:k + 1, :] * gs_s[pl.ds(hr - 1 + k, tm), :]
    v = jnp.dot(hn_s[hr:hr + tm, :], wv_ref[...], preferred_element_type=F32)
    o_ref[...] += jnp.dot((_gelu(gc) * v).astype(BF16), wd_ref[...], preferred_element_type=F32)

    @pl.when(j == nf - 1)
    def _():
        o_ref[...] = x_ref[...] + g2_ref[0] * _rms(o_ref[...], ng3_ref[...])


def _ffn(x2d, modl, ng2, ng3, wup_bf, cw, cb, wdn_bf, seq, tm, tf, tiles_per_batch):
    n, d = x2d.shape
    d_ff = wdn_bf.shape[0]
    nb, nf = seq // tm, d_ff // tf
    hr = BF16_ROWS
    per_t, last = tm // hr, n // hr - 1
    prev = pl.BlockSpec((hr, d), lambda i, j: (jnp.maximum(i * per_t - 1, 0), 0))
    nxt = pl.BlockSpec((hr, d), lambda i, j: (jnp.minimum((i + 1) * per_t, last), 0))
    vec = pl.BlockSpec((1, d), lambda i, j: (0, 0))
    return pl.pallas_call(
        functools.partial(_ffn_kernel, tm=tm, nb=nb, nf=nf),
        grid=(n // tm, nf),
        in_specs=[pl.BlockSpec((tm, d), lambda i, j: (i, 0)), prev, nxt,
                  _mod_spec(d, 3, tiles_per_batch), _mod_spec(d, 4, tiles_per_batch),
                  _mod_spec(d, 5, tiles_per_batch), vec, vec,
                  pl.BlockSpec((d, tf), lambda i, j: (0, j)),
                  pl.BlockSpec((d, tf), lambda i, j: (0, nf + j)),
                  pl.BlockSpec((FFN_CONV, tf), lambda i, j: (0, j)),
                  pl.BlockSpec((1, tf), lambda i, j: (0, j)),
                  pl.BlockSpec((tf, d), lambda i, j: (j, 0))],
        out_specs=pl.BlockSpec((tm, d), lambda i, j: (i, 0)),
        out_shape=jax.ShapeDtypeStruct((n, d), F32),
        scratch_shapes=[pltpu.VMEM((tm + 2 * hr, d), BF16), pltpu.VMEM((tm + 2 * hr, tf), F32)],
        compiler_params=_params("arbitrary", "arbitrary"),
        name="ffn",
    )(x2d, x2d, x2d, modl, modl, modl, ng2, ng3, wup_bf, wup_bf, cw, cb, wdn_bf)


def _rope_tables(seq):
    t = jnp.arange(seq)
    row = (t // GRID_W).astype(F32)
    col = (t % GRID_W).astype(F32)
    inv = ROPE_BASE ** (-jnp.arange(N_FREQ, dtype=F32) / N_FREQ)
    ang_r, ang_c = row[:, None] * inv, col[:, None] * inv
    cos = jnp.concatenate([jnp.cos(ang_r)] * 2 + [jnp.cos(ang_c)] * 2, axis=1)
    sin = jnp.concatenate([-jnp.sin(ang_r), jnp.sin(ang_r), -jnp.sin(ang_c), jnp.sin(ang_c)], axis=1)
    return jnp.tile(cos, (1, 2)), jnp.tile(sin, (1, 2))


def _gate_weights(wa, ba, wi, bi):
    eye = jnp.eye(LRU_BLOCKS, dtype=wa.dtype)
    dense = lambda m: jnp.einsum('nde,nm->ndme', m, eye).reshape(GROUP_W, GROUP_W)
    return (jnp.concatenate([dense(wa), dense(wi)], axis=1).astype(BF16),
            jnp.concatenate([ba, bi])[None, :])


def _row_tile(seq):
    return min(seq, 512)


def _mixers(p, layer, batch, seq, h0f, h0b, prefix, lam_init, want_y):
    p_lru, p_pool, p_fft, qt, k, vt = p
    t = _row_tile(seq)
    hb, st_b = _lru(p_lru, None, layer['cw'], layer['cb'], layer['wg'][1], layer['bg'][1], layer['lam'][1],
                    h0b, batch, seq, t, True)
    y_lru, st_f = _lru(p_lru, hb, layer['cw'], layer['cb'], layer['wg'][0], layer['bg'][0], layer['lam'][0],
                       h0f, batch, seq, t, False)
    ys = None
    if want_y:
        y_pool = _pool(p_pool, layer['wp'], layer['sp'], batch, seq, t)
        if seq % (DFT_N * SUBLANES) == 0:
            y_fft = _fft_big(p_fft, layer['wf'], batch, seq)
        else:
            y_fft = _fft_dense(p_fft, layer['wf'], batch, seq)
        if prefix is not None:
            k_all, vt_all = (jnp.concatenate([pre, cur], axis=2) for pre, cur in zip(prefix, (k, vt)))
        else:
            k_all, vt_all = k, vt
        y_att = _attn(qt, k_all, vt_all, layer['dl'], layer['gs'], batch, seq, t, lam_init)
        ys = (y_lru, y_pool, y_fft, y_att)
    return ys, (st_f, st_b), (k, vt)


def kernel(x, c, ctx, c_ctx, ada_w, ada_b, norm_g, w_in, lru_conv_w, lru_conv_b, lru_wa, lru_ba, lru_wi, lru_bi,
           lru_lam, pool_w, pool_scale, fourier_w, diff_lam, diff_subln_g, w_out, ffn_w_up, ffn_conv_w,
           ffn_conv_b, ffn_w_down):
    batch, seq, d = x.shape
    ctx_len = ctx.shape[1]
    depth = ada_w.shape[0]
    d_ff = ffn_w_down.shape[1]
    tm, tmc = _row_tile(seq), _row_tile(ctx_len)
    tpb = seq // tm
    tf = 512

    c8 = jnp.concatenate([c.astype(F32), c_ctx.astype(F32)[None, :],
                          jnp.zeros((SUBLANES - batch - 1, d), F32)], axis=0)
    mod = _ada(c8, ada_w, ada_b).reshape(depth, SUBLANES * N_MOD, 1, d)
    rope = _rope_tables(seq)
    zeros_state = jnp.zeros((batch, 1, GROUP_W), F32)

    x2 = x.reshape(batch * seq, d)
    xc = ctx.reshape(batch * ctx_len, d)
    for l in range(depth):
        last = l == depth - 1
        lam_init = 0.8 - 0.6 * math.exp(-0.3 * l)
        gates = [_gate_weights(lru_wa[l, dr], lru_ba[l, dr], lru_wi[l, dr], lru_bi[l, dr]) for dr in range(2)]
        layer = dict(cw=lru_conv_w[l], cb=lru_conv_b[l][None, :],
                     wg=[g[0] for g in gates], bg=[g[1] for g in gates],
                     lam=[lru_lam[l, dr][None, :] for dr in range(2)],
                     wp=pool_w[l].astype(BF16), sp=pool_scale[l][None, :], wf=fourier_w[l].astype(BF16),
                     dl=diff_lam[l], gs=diff_subln_g[l][None, :])
        w_in_bf, w_out_bf = w_in[l].astype(BF16), w_out[l].astype(BF16)
        w_up_bf, w_dn_bf = ffn_w_up[l].astype(BF16), ffn_w_down[l].astype(BF16)
        ng = [norm_g[l, k][None, :] for k in range(4)]
        modl = mod[l]
        ffn = functools.partial(_ffn, ng2=ng[2], ng3=ng[3], wup_bf=w_up_bf, cw=ffn_conv_w[l],
                                cb=ffn_conv_b[l][None, :], wdn_bf=w_dn_bf, tf=tf)

        pc = _in_proj(xc, modl, ng[0], w_in_bf, None, batch, ctx_len, tmc, True)
        ysc, (stf, stb), ctx_kv = _mixers(pc, layer, batch, ctx_len, zeros_state, zeros_state, None,
                                          lam_init, not last)
        pz = _in_proj(x2, modl, ng[0], w_in_bf, rope, batch, seq, tm, False)
        ys, _, _ = _mixers(pz, layer, batch, seq, stf[:, :1], stb[:, :1], ctx_kv, lam_init, True)
        x2 = _out_proj(ys, w_out_bf, x2, modl, ng[1], tm, tpb)
        x2 = ffn(x2, modl, seq=seq, tm=tm, tiles_per_batch=tpb)
        if not last:
            xc = _out_proj(ysc, w_out_bf, xc, modl, ng[1], tmc, None)
            xc = ffn(xc, modl, seq=ctx_len, tm=tmc, tiles_per_batch=None)
    return x2.reshape(batch, seq, d)
```

```python
import functools
import math

import numpy as np
import jax
import jax.numpy as jnp
from jax import lax
from jax.experimental import pallas as pl
from jax.experimental.pallas import tpu as pltpu

F32 = jnp.float32
BF16 = jnp.bfloat16

EPS = 1e-6
N_MOD = 6
GROUP_W = 512
LRU_BLOCKS = 8
LRU_CONV = 4
LRU_C = 8.0
SQRT_GUARD = 1e-37
POOL_HALF = (1, 2, 4, 8)
HEADS = 4
HEAD_W = 128
ATT_DH = 64
ATT_TK = 256
ATT_WIDE = 2
ATT_UNROLL = 5
N_FREQ = ATT_DH // 4
ROPE_BASE = 10000.0
GRID_W = 64
FFN_CONV = 3
SUBLANES = 8
BF16_ROWS = 16
VT_ROWS = HEAD_W + BF16_ROWS
DFT_N = 128
FFT_KB = 4
V7X_VMEM_LIMIT = 56 * 1024 * 1024
LOG2E = 1.4426950408889634


def _params(*sem):
    return pltpu.CompilerParams(dimension_semantics=sem, vmem_limit_bytes=V7X_VMEM_LIMIT)


def _rms(xf, g):
    return xf * lax.rsqrt(jnp.mean(xf * xf, axis=-1, keepdims=True) + EPS) * g


def _gelu(x):
    return x * (0.5 * (1.0 + jnp.tanh(0.7978845608028654 * (x + 0.044715 * (x * x * x)))))


def _sigmoid(x):
    return 1.0 / (1.0 + jnp.exp(-x))


def _ada_kernel(c_ref, w_ref, b_ref, o_ref):
    c = c_ref[...]
    s = (c * _sigmoid(c)).astype(BF16)
    o_ref[0] = jnp.dot(s, w_ref[0].astype(BF16), preferred_element_type=F32) + b_ref[0]


def _ada(c8, ada_w, ada_b):
    depth, d, nm = ada_w.shape
    tn = 1024
    return pl.pallas_call(
        _ada_kernel,
        grid=(depth, nm // tn),
        in_specs=[pl.BlockSpec((SUBLANES, d), lambda l, j: (0, 0)),
                  pl.BlockSpec((1, d, tn), lambda l, j: (l, 0, j)),
                  pl.BlockSpec((1, 1, tn), lambda l, j: (l, 0, j))],
        out_specs=pl.BlockSpec((1, SUBLANES, tn), lambda l, j: (l, 0, j)),
        out_shape=jax.ShapeDtypeStruct((depth, SUBLANES, nm), F32),
        compiler_params=_params("arbitrary", "arbitrary"),
        name="ada",
    )(c8, ada_w, ada_b.reshape(depth, 1, nm))


def _mod_spec(d, which, tiles_per_batch):
    if tiles_per_batch is None:
        return pl.BlockSpec((1, 1, d), lambda i, *_: (2 * N_MOD + which, 0, 0))
    return pl.BlockSpec((1, 1, d), lambda i, *_: ((i // tiles_per_batch) * N_MOD + which, 0, 0))


def _inproj_kernel(*refs, use_rope, qscale):
    if use_rope:
        (x_ref, sh_ref, sc_ref, g_ref, w_ref, cos_ref, sin_ref,
         lru_ref, pool_ref, fft_ref, qt_ref, k_ref, vt_ref) = refs
    else:
        x_ref, sh_ref, sc_ref, g_ref, w_ref, lru_ref, pool_ref, fft_ref, qt_ref, k_ref, vt_ref = refs
    w = GROUP_W
    h = _rms(x_ref[...], g_ref[...]) * (1.0 + sc_ref[0]) + sh_ref[0]
    p = jnp.dot(h.astype(BF16), w_ref[...], preferred_element_type=F32)
    lru_ref[...] = p[:, :2 * w]
    pool_ref[...] = p[:, 2 * w:3 * w]
    fft_ref[...] = p[:, 3 * w:4 * w]
    tm = p.shape[0]
    if use_rope:
        cos, sin = cos_ref[...], sin_ref[...]
        lane = lax.broadcasted_iota(jnp.int32, (tm, HEAD_W), 1)
        first = (lane % (2 * N_FREQ)) < N_FREQ

        def rope(v):
            partner = jnp.where(first, pltpu.roll(v, HEAD_W - N_FREQ, 1), pltpu.roll(v, N_FREQ, 1))
            return v * cos + partner * sin
    else:
        rope = lambda v: v
    ones = jnp.ones((VT_ROWS - HEAD_W, ATT_TK), BF16)
    for hd in range(HEADS):
        cols = lambda grp: slice((4 + grp) * w + hd * HEAD_W, (4 + grp) * w + (hd + 1) * HEAD_W)
        qt_ref[0, hd] = (rope(p[:, cols(0)]) * qscale).T.astype(BF16)
        kb = rope(p[:, cols(1)]).astype(BF16)
        vtb = p[:, cols(2)].T.astype(BF16)
        for ck in range(tm // ATT_TK):
            k_ref[0, hd, ck] = kb[ck * ATT_TK:(ck + 1) * ATT_TK, :]
            vt_ref[0, hd, ck, 0:HEAD_W, :] = vtb[:, ck * ATT_TK:(ck + 1) * ATT_TK]
            vt_ref[0, hd, ck, HEAD_W:VT_ROWS, :] = ones


def _in_proj(x2d, modl, ng, w_all, layer_idx, rope, batch, seq, tm, is_ctx):
    n, d = x2d.shape
    d_in = w_all.shape[2]
    w = GROUP_W
    tpb = seq // tm
    nck = tm // ATT_TK
    use_rope = rope is not None
    in_specs = [pl.BlockSpec((tm, d), lambda i: (i, 0)),
                _mod_spec(d, 0, None if is_ctx else tpb),
                _mod_spec(d, 1, None if is_ctx else tpb),
                pl.BlockSpec((1, d), lambda i: (0, 0)),
                pl.BlockSpec((None, d, d_in), lambda i: (layer_idx, 0, 0), pipeline_mode=pl.Buffered(1))]
    args = [x2d, modl, modl, ng, w_all]
    if use_rope:
        in_specs += [pl.BlockSpec((tm, HEAD_W), lambda i: (i % tpb, 0))] * 2
        args += list(rope)
    row = lambda wd: pl.BlockSpec((tm, wd), lambda i: (i, 0))
    return pl.pallas_call(
        functools.partial(_inproj_kernel, use_rope=use_rope, qscale=ATT_DH ** -0.5 * LOG2E),
        grid=(n // tm,),
        in_specs=in_specs,
        out_specs=[row(2 * w), row(w), row(w),
                   pl.BlockSpec((1, HEADS, HEAD_W, tm), lambda i: (i // tpb, 0, 0, i % tpb)),
                   pl.BlockSpec((1, HEADS, nck, ATT_TK, HEAD_W), lambda i: (i // tpb, 0, i % tpb, 0, 0)),
                   pl.BlockSpec((1, HEADS, nck, VT_ROWS, ATT_TK), lambda i: (i // tpb, 0, i % tpb, 0, 0))],
        out_shape=[jax.ShapeDtypeStruct((n, 2 * w), F32), jax.ShapeDtypeStruct((n, w), F32),
                   jax.ShapeDtypeStruct((n, w), F32),
                   jax.ShapeDtypeStruct((batch, HEADS, HEAD_W, seq), BF16),
                   jax.ShapeDtypeStruct((batch, HEADS, seq // ATT_TK, ATT_TK, HEAD_W), BF16),
                   jax.ShapeDtypeStruct((batch, HEADS, seq // ATT_TK, VT_ROWS, ATT_TK), BF16)],
        compiler_params=_params("arbitrary"),
        name="in_proj",
    )(*args)


def _halo_specs(width, col, n_rows, seq, t, blk_of):
    per_b, per_t, last = seq // SUBLANES, t // SUBLANES, n_rows // SUBLANES - 1
    prev = pl.BlockSpec((SUBLANES, width),
                        lambda b, i, *_: (jnp.maximum(b * per_b + blk_of(i) * per_t - 1, 0), col))
    nxt = pl.BlockSpec((SUBLANES, width),
                       lambda b, i, *_: (jnp.minimum(b * per_b + (blk_of(i) + 1) * per_t, last), col))
    return prev, nxt


def _fill_ext(ext_s, x, prev, nxt, blk, nb, t):
    ext_s[0:SUBLANES, :] = jnp.where(blk == 0, 0.0, prev)
    ext_s[SUBLANES:SUBLANES + t, :] = x
    ext_s[SUBLANES + t:2 * SUBLANES + t, :] = jnp.where(blk == nb - 1, 0.0, nxt)


def _lru_kernel(*refs, t, nb, reverse, emit_y):
    if emit_y:
        (ux_ref, prev_ref, next_ref, gate_ref, hb_ref, cw_ref, cb_ref, wg_ref, bg_ref, lam_ref, h0_ref,
         out_ref, st_ref, ext_s, car_s) = refs
    else:
        (ux_ref, prev_ref, next_ref, cw_ref, cb_ref, wg_ref, bg_ref, lam_ref, h0_ref,
         out_ref, st_ref, ext_s, car_s) = refs
    i = pl.program_id(1)
    blk = (nb - 1 - i) if reverse else i
    w = GROUP_W
    ngrp = w // HEAD_W

    @pl.when(i == 0)
    def _():
        car_s[...] = h0_ref[0]

    _fill_ext(ext_s, ux_ref[...], prev_ref[...], next_ref[...], blk, nb, t)
    u = cb_ref[...] + cw_ref[0:1, :] * ext_s[pl.ds(SUBLANES - 2, t), :]
    for k in range(1, LRU_CONV):
        u = u + cw_ref[k:k + 1, :] * ext_s[pl.ds(SUBLANES - 2 + k, t), :]

    g = jnp.dot(u.astype(BF16), wg_ref[...], preferred_element_type=F32) + bg_ref[...]
    r = _sigmoid(g[:, :w])
    ig = _sigmoid(g[:, w:])
    nl = -lam_ref[...]
    softplus = jnp.maximum(nl, 0.0) + jnp.log1p(jnp.exp(-jnp.abs(nl)))
    log_a = (-LRU_C) * r * softplus
    a = jnp.exp(log_a)
    one_m = 1.0 - a * a
    bb = (one_m * lax.rsqrt(jnp.maximum(one_m, SQRT_GUARD))) * (ig * u)

    ng = t // SUBLANES
    rowid = lax.broadcasted_iota(jnp.int32, (1, SUBLANES, HEAD_W), 1)
    c_all = car_s[...]
    c_new, h_cols = [], []
    for cg in range(ngrp):
        av = a[:, cg * HEAD_W:(cg + 1) * HEAD_W].reshape(ng, SUBLANES, HEAD_W)
        bv = bb[:, cg * HEAD_W:(cg + 1) * HEAD_W].reshape(ng, SUBLANES, HEAD_W)
        for dist in (1, 2, 4):
            shift = SUBLANES - dist if reverse else dist
            valid = (rowid < SUBLANES - dist) if reverse else (rowid >= dist)
            b_far = jnp.where(valid, pltpu.roll(bv, shift, 1), 0.0)
            a_far = jnp.where(valid, pltpu.roll(av, shift, 1), 1.0)
            bv = bv + av * b_far
            av = av * a_far
        c = c_all[:, cg * HEAD_W:(cg + 1) * HEAD_W]
        hs = [None] * ng
        for g in (range(ng - 1, -1, -1) if reverse else range(ng)):
            hs[g] = bv[g] + av[g] * c
            c = hs[g][0:1, :] if reverse else hs[g][SUBLANES - 1:SUBLANES, :]
        c_new.append(c)
        h_cols.append(jnp.concatenate(hs, axis=0))
    c = jnp.concatenate(c_new, axis=1)
    car_s[...] = c
    st_ref[0] = jnp.broadcast_to(c, (SUBLANES, w))
    hfull = jnp.concatenate(h_cols, axis=1)
    if emit_y:
        out_ref[...] = (_gelu(gate_ref[...]) * (hfull + hb_ref[...])).astype(out_ref.dtype)
    else:
        out_ref[...] = hfull


def _lru(p_lru, hb, cw, cb, wg, bg, lam, h0, batch, seq, t, reverse):
    n = p_lru.shape[0]
    w = GROUP_W
    nb = seq // t
    emit_y = not reverse
    blk_of = (lambda i: nb - 1 - i) if reverse else (lambda i: i)
    main = lambda col: pl.BlockSpec((t, w), lambda b, i: (b * nb + blk_of(i), col))
    prev, nxt = _halo_specs(w, 0, n, seq, t, blk_of)
    const = lambda shape: pl.BlockSpec(shape, lambda b, i: (0,) * len(shape))
    in_specs = [main(0), prev, nxt]
    args = [p_lru, p_lru, p_lru]
    if emit_y:
        in_specs += [main(1), main(0)]
        args += [p_lru, hb]
    in_specs += [const((LRU_CONV, w)), const((1, w)), const((w, 2 * w)), const((1, 2 * w)), const((1, w)),
                 pl.BlockSpec((1, 1, w), lambda b, i: (b, 0, 0))]
    args += [cw, cb, wg, bg, lam, h0]
    scratch = [pltpu.VMEM((t + 2 * SUBLANES, w), F32), pltpu.VMEM((1, w), F32)]
    return pl.pallas_call(
        functools.partial(_lru_kernel, t=t, nb=nb, reverse=reverse, emit_y=emit_y),
        grid=(batch, nb),
        in_specs=in_specs,
        out_specs=[pl.BlockSpec((t, w), lambda b, i: (b * nb + blk_of(i), 0)),
                   pl.BlockSpec((1, SUBLANES, w), lambda b, i: (b, 0, 0))],
        out_shape=[jax.ShapeDtypeStruct((n, w), BF16 if emit_y else F32),
                   jax.ShapeDtypeStruct((batch, SUBLANES, w), F32)],
        scratch_shapes=scratch,
        compiler_params=_params("arbitrary", "arbitrary"),
        name="lru_fwd" if emit_y else "lru_bwd",
    )(*args)


def _pool_kernel(x_ref, prev_ref, next_ref, wp_ref, sp_ref, o_ref, ext_s, *, t, nb, seq):
    i = pl.program_id(1)
    x = x_ref[...]
    _fill_ext(ext_s, x, prev_ref[...], next_ref[...], i, nb, t)
    pos = i * t + lax.broadcasted_iota(jnp.int32, (t, HEAD_W), 0)
    for g, hw in enumerate(POOL_HALF):
        cols = slice(g * HEAD_W, (g + 1) * HEAD_W)
        s = ext_s[pl.ds(SUBLANES - hw, t), cols]
        for dlt in range(-hw + 1, hw):
            s = s + ext_s[pl.ds(SUBLANES + dlt, t), cols]
        cnt = (jnp.minimum(pos + hw - 1, seq - 1) - jnp.maximum(pos - hw, 0) + 1).astype(F32)
        dif = s / cnt - x[:, cols]
        y = jnp.dot(dif.astype(BF16), wp_ref[g], preferred_element_type=F32) * sp_ref[:, cols]
        o_ref[:, cols] = y.astype(o_ref.dtype)


def _pool(p_pool, wp_bf, sp, batch, seq, t):
    n, w = p_pool.shape
    nb = seq // t
    prev, nxt = _halo_specs(w, 0, n, seq, t, lambda i: i)
    return pl.pallas_call(
        functools.partial(_pool_kernel, t=t, nb=nb, seq=seq),
        grid=(batch, nb),
        in_specs=[pl.BlockSpec((t, w), lambda b, i: (b * nb + i, 0)), prev, nxt,
                  pl.BlockSpec(wp_bf.shape, lambda b, i: (0, 0, 0)),
                  pl.BlockSpec((1, w), lambda b, i: (0, 0))],
        out_specs=pl.BlockSpec((t, w), lambda b, i: (b * nb + i, 0)),
        out_shape=jax.ShapeDtypeStruct((n, w), BF16),
        scratch_shapes=[pltpu.VMEM((t + 2 * SUBLANES, w), F32)],
        compiler_params=_params("arbitrary", "arbitrary"),
        name="pool",
    )(p_pool, p_pool, p_pool, wp_bf, sp)


def _cos_sin(n):
    ang = 2.0 * np.pi * np.outer(np.arange(n), np.arange(n)) / n
    return np.cos(ang), np.sin(ang)


def _ffta_kernel(x_ref, f1_ref, twc_ref, tws_ref, wc_ref, o_ref, *, t1n):
    w = GROUP_W
    hh = jnp.dot(f1_ref[...], x_ref[0].astype(BF16), preferred_element_type=F32)
    hr, hi = hh[:DFT_N], hh[DFT_N:]
    twc, tws = twc_ref[0], tws_ref[0]
    parts_r, parts_i = [], []
    for t1 in range(t1n):
        cw, sw = twc[:, t1:t1 + 1], tws[:, t1:t1 + 1]
        ar, ai = hr[:, t1 * w:(t1 + 1) * w], hi[:, t1 * w:(t1 + 1) * w]
        br = ar * cw + ai * sw
        bi = ai * cw - ar * sw
        for h in range(HEADS):
            parts_r.append(br[:, h * HEAD_W:(h + 1) * HEAD_W])
            parts_i.append(bi[:, h * HEAD_W:(h + 1) * HEAD_W])
    ab = jnp.concatenate([jnp.concatenate(parts_r, axis=0), jnp.concatenate(parts_i, axis=0)], axis=1)
    g = jnp.dot(ab.astype(BF16), wc_ref[...], preferred_element_type=F32)
    for t1 in range(t1n):
        for h in range(HEADS):
            rows = slice((t1 * HEADS + h) * DFT_N, (t1 * HEADS + h + 1) * DFT_N)
            base = t1 * 2 * w + h * HEAD_W
            o_ref[0, :, base:base + HEAD_W] = g[rows, :HEAD_W]
            o_ref[0, :, base + w:base + w + HEAD_W] = g[rows, HEAD_W:]


def _fftb_kernel(g_ref, f3_ref, wf_ref, o_ref):
    w = GROUP_W
    for kk in range(g_ref.shape[1]):
        blk = g_ref[0, kk]
        gg = jnp.concatenate([blk[:, :w], blk[:, w:]], axis=0).astype(BF16)
        y = jnp.dot(f3_ref[...], gg, preferred_element_type=F32)
        o_ref[0, :, kk * w:(kk + 1) * w] = jnp.dot(y.astype(BF16), wf_ref[...],
                                                   preferred_element_type=F32).astype(o_ref.dtype)


def _fft_big(p_fft, wf_bf, batch, seq):
    w = GROUP_W
    n1 = seq // DFT_N
    t1n = SUBLANES
    c1, s1 = _cos_sin(DFT_N)
    cn, sn = _cos_sin(n1)
    sc = 1.0 / math.sqrt(DFT_N)
    f1 = jnp.asarray(np.concatenate([c1, -s1], axis=0) * sc, BF16)
    wc = jnp.asarray(np.block([[c1, -s1], [s1, c1]]) * sc, BF16)
    f3 = jnp.asarray(np.concatenate([cn, sn], axis=1) / math.sqrt(n1), BF16)
    ang = 2.0 * np.pi * np.outer(np.arange(DFT_N), np.arange(n1)) / seq
    tw = lambda m: jnp.asarray(m.reshape(DFT_N, n1 // t1n, t1n).transpose(1, 0, 2), F32)
    g = pl.pallas_call(
        functools.partial(_ffta_kernel, t1n=t1n),
        grid=(batch, n1 // t1n),
        in_specs=[pl.BlockSpec((1, DFT_N, t1n * w), lambda b, j: (b, 0, j)),
                  pl.BlockSpec(f1.shape, lambda b, j: (0, 0)),
                  pl.BlockSpec((1, DFT_N, t1n), lambda b, j: (j, 0, 0)),
                  pl.BlockSpec((1, DFT_N, t1n), lambda b, j: (j, 0, 0)),
                  pl.BlockSpec(wc.shape, lambda b, j: (0, 0))],
        out_specs=pl.BlockSpec((1, DFT_N, t1n * 2 * w), lambda b, j: (b, 0, j)),
        out_shape=jax.ShapeDtypeStruct((batch, DFT_N, n1 * 2 * w), F32),
        compiler_params=_params("arbitrary", "arbitrary"),
        name="fft_a",
    )(p_fft.reshape(batch, DFT_N, n1 * w), f1, tw(np.cos(ang)), tw(np.sin(ang)), wc)
    y = pl.pallas_call(
        _fftb_kernel,
        grid=(batch, DFT_N // FFT_KB),
        in_specs=[pl.BlockSpec((1, FFT_KB, n1, 2 * w), lambda b, k: (b, k, 0, 0)),
                  pl.BlockSpec(f3.shape, lambda b, k: (0, 0)),
                  pl.BlockSpec((w, w), lambda b, k: (0, 0))],
        out_specs=pl.BlockSpec((1, n1, FFT_KB * w), lambda b, k: (b, 0, k)),
        out_shape=jax.ShapeDtypeStruct((batch, n1, DFT_N * w), BF16),
        compiler_params=_params("arbitrary", "arbitrary"),
        name="fft_b",
    )(g.reshape(batch, DFT_N, n1, 2 * w), f3, wf_bf)
    return y.reshape(batch * seq, w)


def _fftd_kernel(x_ref, cs_ref, fl_ref, wf_ref, o_ref):
    xb = x_ref[...].astype(BF16)
    pa, pb = [], []
    for h in range(HEADS):
        z = jnp.dot(xb[:, h * HEAD_W:(h + 1) * HEAD_W], cs_ref[...], preferred_element_type=F32)
        pa.append(z[:, :HEAD_W])
        pb.append(z[:, HEAD_W:])
    ab = jnp.concatenate([jnp.concatenate(pa, axis=1), jnp.concatenate(pb, axis=1)], axis=0).astype(BF16)
    y = jnp.dot(fl_ref[...], ab, preferred_element_type=F32)
    o_ref[...] = jnp.dot(y.astype(BF16), wf_ref[...], preferred_element_type=F32).astype(o_ref.dtype)


def _fft_dense(p_fft, wf_bf, batch, seq):
    w = GROUP_W
    c1, s1 = _cos_sin(DFT_N)
    cl, sl = _cos_sin(seq)
    cs = jnp.asarray(np.concatenate([c1, s1], axis=1) / math.sqrt(DFT_N), BF16)
    fl = jnp.asarray(np.concatenate([cl, -sl], axis=1) / math.sqrt(seq), BF16)
    return pl.pallas_call(
        _fftd_kernel,
        grid=(batch,),
        in_specs=[pl.BlockSpec((seq, w), lambda b: (b, 0)),
                  pl.BlockSpec(cs.shape, lambda b: (0, 0)),
                  pl.BlockSpec(fl.shape, lambda b: (0, 0)),
                  pl.BlockSpec((w, w), lambda b: (0, 0))],
        out_specs=pl.BlockSpec((seq, w), lambda b: (b, 0)),
        out_shape=jax.ShapeDtypeStruct((batch * seq, w), BF16),
        compiler_params=_params("arbitrary"),
        name="fft_dense",
    )(p_fft, cs, fl, wf_bf)


def _attn_kernel(*refs, n, n_pre, lam_init):
    if n_pre:
        qt_ref, k_ref, vt_ref, kp_ref, vtp_ref, dl_ref, gs_ref, o_ref, s_s, cm_s, acc_s = refs
    else:
        qt_ref, k_ref, vt_ref, dl_ref, gs_ref, o_ref, s_s, cm_s, acc_s = refs
        kp_ref = vtp_ref = None
    qt = qt_ref[0, 0]
    tq = qt.shape[1]
    row = lax.broadcasted_iota(jnp.int32, qt.shape, 0)
    zero = jnp.zeros_like(qt)
    qs = (jnp.where(row < ATT_DH, qt, zero), jnp.where(row >= ATT_DH, qt, zero))
    acc_s[...] = jnp.zeros_like(acc_s)

    tk = k_ref.shape[3]

    def stage_a(c, buf, width, pre=False):
        kr = kp_ref if pre else k_ref
        for j in range(2):
            cm = None
            for part in range(width):
                s = jnp.dot(kr[0, 0, c + part], qs[j], preferred_element_type=F32)
                s_s[buf, j, part * tk:(part + 1) * tk, :] = s
                top = jnp.max(s, axis=0, keepdims=True)
                cm = top if cm is None else jnp.maximum(cm, top)
            cm_s[buf, j] = cm

    def stage_b(c, buf, width, ms, pre=False):
        vr = vtp_ref if pre else vt_ref
        vtb = jnp.concatenate([vr[0, 0, c + part] for part in range(width)], axis=1)
        out = []
        for j in range(2):
            mn = jnp.maximum(ms[j], cm_s[buf, j])
            alpha = jnp.exp2(ms[j] - mn)
            p = jnp.exp2(s_s[buf, j, 0:width * tk, :] - mn)
            acc_s[j] = alpha * acc_s[j] + jnp.dot(vtb, p.astype(BF16), preferred_element_type=F32)
            out.append(mn)
        return tuple(out)

    neg = jnp.full((1, tq), -1e30, F32)
    base = n % ATT_WIDE
    nw = n // ATT_WIDE
    first = lambda w: base + ATT_WIDE * w
    order = ([(first(w), ATT_WIDE, False) for w in range(nw)] + [(c, 1, False) for c in range(base)]
             + [(c, 1, True) for c in range(n_pre)])

    def pair(i, ms):
        stage_a(first(2 * i + 1), 1, ATT_WIDE)
        ms = stage_b(first(2 * i), 0, ATT_WIDE, ms)
        stage_a(first(2 * i + 2), 0, ATT_WIDE)
        return stage_b(first(2 * i + 1), 1, ATT_WIDE, ms)

    stage_a(order[0][0], 0, order[0][1], order[0][2])
    trips = max(nw - 1, 0) // 2
    ms = (neg, neg)
    if trips > 0:
        ms = lax.fori_loop(0, trips, pair, ms, unroll=math.gcd(trips, ATT_UNROLL))
    tail = order[2 * trips:]
    for idx, (c, width, pre) in enumerate(tail):
        step = 2 * trips + idx
        if idx + 1 < len(tail):
            stage_a(tail[idx + 1][0], (step + 1) % 2, tail[idx + 1][1], tail[idx + 1][2])
        ms = stage_b(c, step % 2, width, ms, pre)
    dl = dl_ref[...]
    lam = (jnp.exp(jnp.sum(dl[0:1] * dl[1:2], axis=-1, keepdims=True))
           - jnp.exp(jnp.sum(dl[2:3] * dl[3:4], axis=-1, keepdims=True)) + lam_init)
    o0 = acc_s[0, 0:HEAD_W, :] / acc_s[0, HEAD_W:HEAD_W + 1, :]
    o1 = acc_s[1, 0:HEAD_W, :] / acc_s[1, HEAD_W:HEAD_W + 1, :]
    ot = o0 - lam * o1
    o_ref[...] = (_rms(ot.T, gs_ref[...]) * (1.0 - lam_init)).astype(o_ref.dtype)


def _attn(qt, k, vt, prefix, dl, gs, batch, seq, tq, lam_init):
    nq = seq // tq
    n, tk = k.shape[2], k.shape[3]
    kv_spec = lambda arr: pl.BlockSpec((1, 1) + arr.shape[2:], lambda b, h, i: (b, h, 0, 0, 0))
    kv = [k, vt] + (list(prefix) if prefix is not None else [])
    n_pre = prefix[0].shape[2] if prefix is not None else 0
    return pl.pallas_call(
        functools.partial(_attn_kernel, n=n, n_pre=n_pre, lam_init=lam_init),
        grid=(batch, HEADS, nq),
        in_specs=[pl.BlockSpec((1, 1, HEAD_W, tq), lambda b, h, i: (b, h, 0, i))]
                 + [kv_spec(arr) for arr in kv]
                 + [pl.BlockSpec(dl.shape, lambda b, h, i: (0, 0)),
                    pl.BlockSpec((1, HEAD_W), lambda b, h, i: (0, 0))],
        out_specs=pl.BlockSpec((tq, HEAD_W), lambda b, h, i: (b * nq + i, h)),
        out_shape=jax.ShapeDtypeStruct((batch * seq, GROUP_W), BF16),
        scratch_shapes=[pltpu.VMEM((2, 2, ATT_WIDE * tk, tq), F32), pltpu.VMEM((2, 2, 1, tq), F32),
                        pltpu.VMEM((2, VT_ROWS, tq), F32)],
        compiler_params=_params("arbitrary", "arbitrary", "arbitrary"),
        name="attn",
    )(qt, *kv, dl, gs)


def _outproj_kernel(yl_ref, yp_ref, yf_ref, ya_ref, w_ref, x_ref, g_ref, ng_ref, o_ref):
    w = GROUP_W
    y = jnp.dot(yl_ref[...], w_ref[0:w, :], preferred_element_type=F32)
    for j, r in enumerate((yp_ref, yf_ref, ya_ref), start=1):
        y = y + jnp.dot(r[...], w_ref[j * w:(j + 1) * w, :], preferred_element_type=F32)
    o_ref[...] = x_ref[...] + g_ref[0] * _rms(y, ng_ref[...])


def _out_proj(ys, w_all, layer_idx, x2d, modl, ng, tm, tiles_per_batch):
    n, d = x2d.shape
    ytile = pl.BlockSpec((tm, GROUP_W), lambda i: (i, 0))
    return pl.pallas_call(
        _outproj_kernel,
        grid=(n // tm,),
        in_specs=[ytile, ytile, ytile, ytile,
                  pl.BlockSpec((None,) + w_all.shape[1:], lambda i: (layer_idx, 0, 0),
                               pipeline_mode=pl.Buffered(1)),
                  pl.BlockSpec((tm, d), lambda i: (i, 0)),
                  _mod_spec(d, 2, tiles_per_batch),
                  pl.BlockSpec((1, d), lambda i: (0, 0))],
        out_specs=pl.BlockSpec((tm, d), lambda i: (i, 0)),
        out_shape=jax.ShapeDtypeStruct((n, d), F32),
        compiler_params=_params("arbitrary"),
        name="out_proj",
    )(*ys, w_all, x2d, modl, ng)


def _ffn_kernel(x_ref, prev_ref, next_ref, sh_ref, sc_ref, g2_ref, ng2_ref, ng3_ref, wg_ref, wv_ref, cw_ref,
                cb_ref, wd_ref, o_ref, hn_s, gs_s, *, tm, nb, nf):
    i = pl.program_id(0) % nb
    j = pl.program_id(1)
    hr = BF16_ROWS

    @pl.when(j == 0)
    def _():
        norm = lambda v: (_rms(v, ng2_ref[...]) * (1.0 + sc_ref[0]) + sh_ref[0]).astype(BF16)
        hn_s[0:hr, :] = norm(prev_ref[...])
        hn_s[hr:hr + tm, :] = norm(x_ref[...])
        hn_s[hr + tm:2 * hr + tm, :] = norm(next_ref[...])
        o_ref[...] = jnp.zeros_like(o_ref)

    g = jnp.dot(hn_s[...], wg_ref[...], preferred_element_type=F32)
    gs_s[0:hr, :] = jnp.where(i == 0, 0.0, g[0:hr])
    gs_s[hr:hr + tm, :] = g[hr:hr + tm]
    gs_s[hr + tm:2 * hr + tm, :] = jnp.where(i == nb - 1, 0.0, g[hr + tm:2 * hr + tm])
    gc = cb_ref[...] + cw_ref[0:1, :] * gs_s[pl.ds(hr - 1, tm), :]
    for k in range(1, FFN_CONV):
        gc = gc + cw_ref[k:k + 1, :] * gs_s[pl.ds(hr - 1 + k, tm), :]
    v = jnp.dot(hn_s[hr:hr + tm, :], wv_ref[...], preferred_element_type=F32)
    o_ref[...] += jnp.dot((_gelu(gc) * v).astype(BF16), wd_ref[...], preferred_element_type=F32)

    @pl.when(j == nf - 1)
    def _():
        o_ref[...] = x_ref[...] + g2_ref[0] * _rms(o_ref[...], ng3_ref[...])


def _ffn(x2d, modl, ng2, ng3, wup_all, wdn_all, layer_idx, cw, cb, seq, tm, tf, tiles_per_batch):
    n, d = x2d.shape
    d_ff = wdn_all.shape[1]
    nb, nf = seq // tm, d_ff // tf
    hr = BF16_ROWS
    per_t, last = tm // hr, n // hr - 1
    prev = pl.BlockSpec((hr, d), lambda i, j: (jnp.maximum(i * per_t - 1, 0), 0))
    nxt = pl.BlockSpec((hr, d), lambda i, j: (jnp.minimum((i + 1) * per_t, last), 0))
    vec = pl.BlockSpec((1, d), lambda i, j: (0, 0))
    return pl.pallas_call(
        functools.partial(_ffn_kernel, tm=tm, nb=nb, nf=nf),
        grid=(n // tm, nf),
        in_specs=[pl.BlockSpec((tm, d), lambda i, j: (i, 0)), prev, nxt,
                  _mod_spec(d, 3, tiles_per_batch), _mod_spec(d, 4, tiles_per_batch),
                  _mod_spec(d, 5, tiles_per_batch), vec, vec,
                  pl.BlockSpec((None, d, tf), lambda i, j: (layer_idx, 0, j)),
                  pl.BlockSpec((None, d, tf), lambda i, j: (layer_idx, 0, nf + j)),
                  pl.BlockSpec((FFN_CONV, tf), lambda i, j: (0, j)),
                  pl.BlockSpec((1, tf), lambda i, j: (0, j)),
                  pl.BlockSpec((None, tf, d), lambda i, j: (layer_idx, j, 0))],
        out_specs=pl.BlockSpec((tm, d), lambda i, j: (i, 0)),
        out_shape=jax.ShapeDtypeStruct((n, d), F32),
        scratch_shapes=[pltpu.VMEM((tm + 2 * hr, d), BF16), pltpu.VMEM((tm + 2 * hr, tf), F32)],
        compiler_params=_params("arbitrary", "arbitrary"),
        name="ffn",
    )(x2d, x2d, x2d, modl, modl, modl, ng2, ng3, wup_all, wup_all, cw, cb, wdn_all)


def _rope_tables(seq):
    t = jnp.arange(seq)
    row = (t // GRID_W).astype(F32)
    col = (t % GRID_W).astype(F32)
    inv = ROPE_BASE ** (-jnp.arange(N_FREQ, dtype=F32) / N_FREQ)
    ang_r, ang_c = row[:, None] * inv, col[:, None] * inv
    cos = jnp.concatenate([jnp.cos(ang_r)] * 2 + [jnp.cos(ang_c)] * 2, axis=1)
    sin = jnp.concatenate([-jnp.sin(ang_r), jnp.sin(ang_r), -jnp.sin(ang_c), jnp.sin(ang_c)], axis=1)
    return jnp.tile(cos, (1, 2)), jnp.tile(sin, (1, 2))


def _gate_weights(wa, ba, wi, bi):
    eye = jnp.eye(LRU_BLOCKS, dtype=wa.dtype)
    dense = lambda m: jnp.einsum('nde,nm->ndme', m, eye).reshape(GROUP_W, GROUP_W)
    return (jnp.concatenate([dense(wa), dense(wi)], axis=1).astype(BF16),
            jnp.concatenate([ba, bi])[None, :])


def _row_tile(seq):
    return min(seq, 512)


def _mixers(p, layer, batch, seq, h0f, h0b, prefix, lam_init, want_y):
    p_lru, p_pool, p_fft, qt, k, vt = p
    t = _row_tile(seq)
    hb, st_b = _lru(p_lru, None, layer['cw'], layer['cb'], layer['wg'][1], layer['bg'][1], layer['lam'][1],
                    h0b, batch, seq, t, True)
    y_lru, st_f = _lru(p_lru, hb, layer['cw'], layer['cb'], layer['wg'][0], layer['bg'][0], layer['lam'][0],
                       h0f, batch, seq, t, False)
    ys = None
    if want_y:
        y_pool = _pool(p_pool, layer['wp'], layer['sp'], batch, seq, t)
        if seq % (DFT_N * SUBLANES) == 0:
            y_fft = _fft_big(p_fft, layer['wf'], batch, seq)
        else:
            y_fft = _fft_dense(p_fft, layer['wf'], batch, seq)
        y_att = _attn(qt, k, vt, prefix, layer['dl'], layer['gs'], batch, seq, t, lam_init)
        ys = (y_lru, y_pool, y_fft, y_att)
    return ys, (st_f, st_b), (k, vt)


def kernel(x, c, ctx, c_ctx, ada_w, ada_b, norm_g, w_in, lru_conv_w, lru_conv_b, lru_wa, lru_ba, lru_wi, lru_bi,
           lru_lam, pool_w, pool_scale, fourier_w, diff_lam, diff_subln_g, w_out, ffn_w_up, ffn_conv_w,
           ffn_conv_b, ffn_w_down):
    batch, seq, d = x.shape
    ctx_len = ctx.shape[1]
    depth = ada_w.shape[0]
    d_ff = ffn_w_down.shape[1]
    tm, tmc = _row_tile(seq), _row_tile(ctx_len)
    tpb = seq // tm
    tf = 512

    c8 = jnp.concatenate([c.astype(F32), c_ctx.astype(F32)[None, :],
                          jnp.zeros((SUBLANES - batch - 1, d), F32)], axis=0)
    mod = _ada(c8, ada_w, ada_b).reshape(depth, SUBLANES * N_MOD, 1, d)
    rope = _rope_tables(seq)
    zeros_state = jnp.zeros((batch, 1, GROUP_W), F32)
    w_in_all, w_out_all = w_in.astype(BF16), w_out.astype(BF16)
    w_up_all, w_dn_all = ffn_w_up.astype(BF16), ffn_w_down.astype(BF16)

    x2 = x.reshape(batch * seq, d)
    xc = ctx.reshape(batch * ctx_len, d)
    for l in range(depth):
        last = l == depth - 1
        lam_init = 0.8 - 0.6 * math.exp(-0.3 * l)
        gates = [_gate_weights(lru_wa[l, dr], lru_ba[l, dr], lru_wi[l, dr], lru_bi[l, dr]) for dr in range(2)]
        layer = dict(cw=lru_conv_w[l], cb=lru_conv_b[l][None, :],
                     wg=[g[0] for g in gates], bg=[g[1] for g in gates],
                     lam=[lru_lam[l, dr][None, :] for dr in range(2)],
                     wp=pool_w[l].astype(BF16), sp=pool_scale[l][None, :], wf=fourier_w[l].astype(BF16),
                     dl=diff_lam[l], gs=diff_subln_g[l][None, :])
        ng = [norm_g[l, k][None, :] for k in range(4)]
        modl = mod[l]
        ffn = functools.partial(_ffn, ng2=ng[2], ng3=ng[3], wup_all=w_up_all, wdn_all=w_dn_all, layer_idx=l,
                                cw=ffn_conv_w[l], cb=ffn_conv_b[l][None, :], tf=tf)

        pc = _in_proj(xc, modl, ng[0], w_in_all, l, None, batch, ctx_len, tmc, True)
        ysc, (stf, stb), ctx_kv = _mixers(pc, layer, batch, ctx_len, zeros_state, zeros_state, None,
                                          lam_init, not last)
        pz = _in_proj(x2, modl, ng[0], w_in_all, l, rope, batch, seq, tm, False)
        ys, _, _ = _mixers(pz, layer, batch, seq, stf[:, :1], stb[:, :1], ctx_kv, lam_init, True)
        x2 = _out_proj(ys, w_out_all, l, x2, modl, ng[1], tm, tpb)
        x2 = ffn(x2, modl, seq=seq, tm=tm, tiles_per_batch=tpb)
        if not last:
            xc = _out_proj(ysc, w_out_all, l, xc, modl, ng[1], tmc, None)
            xc = ffn(xc, modl, seq=ctx_len, tm=tmc, tiles_per_batch=None)
    return x2.reshape(batch, seq, d)
```

```python
import functools
import math

import numpy as np
import jax
import jax.numpy as jnp
from jax import lax
from jax.experimental import pallas as pl
from jax.experimental.pallas import tpu as pltpu

F32 = jnp.float32
BF16 = jnp.bfloat16

EPS = 1e-6
N_MOD = 6
GROUP_W = 512
LRU_BLOCKS = 8
LRU_CONV = 4
LRU_C = 8.0
SQRT_GUARD = 1e-37
POOL_HALF = (1, 2, 4, 8)
HEADS = 4
HEAD_W = 128
ATT_DH = 64
ATT_TK = 256
ATT_WIDE = 2
ATT_UNROLL = 5
N_FREQ = ATT_DH // 4
ROPE_BASE = 10000.0
GRID_W = 64
FFN_CONV = 3
SUBLANES = 8
BF16_ROWS = 16
VT_ROWS = HEAD_W + BF16_ROWS
DFT_N = 128
FFT_KB = 4
V7X_VMEM_LIMIT = 56 * 1024 * 1024
LOG2E = 1.4426950408889634


def _params(*sem):
    return pltpu.CompilerParams(dimension_semantics=sem, vmem_limit_bytes=V7X_VMEM_LIMIT)


def _rms(xf, g):
    return xf * lax.rsqrt(jnp.mean(xf * xf, axis=-1, keepdims=True) + EPS) * g


def _gelu(x):
    return x * (0.5 * (1.0 + jnp.tanh(0.7978845608028654 * (x + 0.044715 * (x * x * x)))))


def _sigmoid(x):
    return 1.0 / (1.0 + jnp.exp(-x))


def _ada_kernel(c_ref, w_ref, b_ref, o_ref):
    c = c_ref[...]
    s = (c * _sigmoid(c)).astype(BF16)
    o_ref[0] = jnp.dot(s, w_ref[0].astype(BF16), preferred_element_type=F32) + b_ref[0]


def _ada(c8, ada_w, ada_b):
    depth, d, nm = ada_w.shape
    tn = 1024
    return pl.pallas_call(
        _ada_kernel,
        grid=(depth, nm // tn),
        in_specs=[pl.BlockSpec((SUBLANES, d), lambda l, j: (0, 0)),
                  pl.BlockSpec((1, d, tn), lambda l, j: (l, 0, j)),
                  pl.BlockSpec((1, 1, tn), lambda l, j: (l, 0, j))],
        out_specs=pl.BlockSpec((1, SUBLANES, tn), lambda l, j: (l, 0, j)),
        out_shape=jax.ShapeDtypeStruct((depth, SUBLANES, nm), F32),
        compiler_params=_params("arbitrary", "arbitrary"),
        name="ada",
    )(c8, ada_w, ada_b.reshape(depth, 1, nm))


def _mod_spec(d, which, tiles_per_batch):
    if tiles_per_batch is None:
        return pl.BlockSpec((1, 1, d), lambda i, *_: (2 * N_MOD + which, 0, 0))
    return pl.BlockSpec((1, 1, d), lambda i, *_: ((i // tiles_per_batch) * N_MOD + which, 0, 0))


def _inproj_kernel(*refs, use_rope, qscale):
    if use_rope:
        (x_ref, sh_ref, sc_ref, g_ref, w_ref, cos_ref, sin_ref,
         lru_ref, pool_ref, fft_ref, qt_ref, k_ref, vt_ref) = refs
    else:
        x_ref, sh_ref, sc_ref, g_ref, w_ref, lru_ref, pool_ref, fft_ref, qt_ref, k_ref, vt_ref = refs
    w = GROUP_W
    h = _rms(x_ref[...], g_ref[...]) * (1.0 + sc_ref[0]) + sh_ref[0]
    p = jnp.dot(h.astype(BF16), w_ref[...], preferred_element_type=F32)
    lru_ref[...] = p[:, :2 * w]
    pool_ref[...] = p[:, 2 * w:3 * w]
    fft_ref[...] = p[:, 3 * w:4 * w].astype(BF16)
    tm = p.shape[0]
    if use_rope:
        cos, sin = cos_ref[...], sin_ref[...]
        lane = lax.broadcasted_iota(jnp.int32, (tm, HEAD_W), 1)
        first = (lane % (2 * N_FREQ)) < N_FREQ

        def rope(v):
            partner = jnp.where(first, pltpu.roll(v, HEAD_W - N_FREQ, 1), pltpu.roll(v, N_FREQ, 1))
            return v * cos + partner * sin
    else:
        rope = lambda v: v
    ones = jnp.ones((VT_ROWS - HEAD_W, ATT_TK), BF16)
    for hd in range(HEADS):
        cols = lambda grp: slice((4 + grp) * w + hd * HEAD_W, (4 + grp) * w + (hd + 1) * HEAD_W)
        qt_ref[0, hd] = (rope(p[:, cols(0)]) * qscale).T.astype(BF16)
        kb = rope(p[:, cols(1)]).astype(BF16)
        vtb = p[:, cols(2)].T.astype(BF16)
        for ck in range(tm // ATT_TK):
            k_ref[0, hd, ck] = kb[ck * ATT_TK:(ck + 1) * ATT_TK, :]
            vt_ref[0, hd, ck, 0:HEAD_W, :] = vtb[:, ck * ATT_TK:(ck + 1) * ATT_TK]
            vt_ref[0, hd, ck, HEAD_W:VT_ROWS, :] = ones


def _in_proj(x2d, modl, ng, w_all, layer_idx, rope, batch, seq, tm, is_ctx):
    n, d = x2d.shape
    d_in = w_all.shape[2]
    w = GROUP_W
    tpb = seq // tm
    nck = tm // ATT_TK
    use_rope = rope is not None
    in_specs = [pl.BlockSpec((tm, d), lambda i: (i, 0)),
                _mod_spec(d, 0, None if is_ctx else tpb),
                _mod_spec(d, 1, None if is_ctx else tpb),
                pl.BlockSpec((1, d), lambda i: (0, 0)),
                pl.BlockSpec((None, d, d_in), lambda i: (layer_idx, 0, 0), pipeline_mode=pl.Buffered(1))]
    args = [x2d, modl, modl, ng, w_all]
    if use_rope:
        in_specs += [pl.BlockSpec((tm, HEAD_W), lambda i: (i % tpb, 0))] * 2
        args += list(rope)
    row = lambda wd: pl.BlockSpec((tm, wd), lambda i: (i, 0))
    return pl.pallas_call(
        functools.partial(_inproj_kernel, use_rope=use_rope, qscale=ATT_DH ** -0.5 * LOG2E),
        grid=(n // tm,),
        in_specs=in_specs,
        out_specs=[row(2 * w), row(w), row(w),
                   pl.BlockSpec((1, HEADS, HEAD_W, tm), lambda i: (i // tpb, 0, 0, i % tpb)),
                   pl.BlockSpec((1, HEADS, nck, ATT_TK, HEAD_W), lambda i: (i // tpb, 0, i % tpb, 0, 0)),
                   pl.BlockSpec((1, HEADS, nck, VT_ROWS, ATT_TK), lambda i: (i // tpb, 0, i % tpb, 0, 0))],
        out_shape=[jax.ShapeDtypeStruct((n, 2 * w), F32), jax.ShapeDtypeStruct((n, w), F32),
                   jax.ShapeDtypeStruct((n, w), BF16),
                   jax.ShapeDtypeStruct((batch, HEADS, HEAD_W, seq), BF16),
                   jax.ShapeDtypeStruct((batch, HEADS, seq // ATT_TK, ATT_TK, HEAD_W), BF16),
                   jax.ShapeDtypeStruct((batch, HEADS, seq // ATT_TK, VT_ROWS, ATT_TK), BF16)],
        compiler_params=_params("arbitrary"),
        name="in_proj",
    )(*args)


def _halo_specs(width, col, n_rows, seq, t, blk_of):
    per_b, per_t, last = seq // SUBLANES, t // SUBLANES, n_rows // SUBLANES - 1
    prev = pl.BlockSpec((SUBLANES, width),
                        lambda b, i, *_: (jnp.maximum(b * per_b + blk_of(i) * per_t - 1, 0), col))
    nxt = pl.BlockSpec((SUBLANES, width),
                       lambda b, i, *_: (jnp.minimum(b * per_b + (blk_of(i) + 1) * per_t, last), col))
    return prev, nxt


def _fill_ext(ext_s, x, prev, nxt, blk, nb, t):
    ext_s[0:SUBLANES, :] = jnp.where(blk == 0, 0.0, prev)
    ext_s[SUBLANES:SUBLANES + t, :] = x
    ext_s[SUBLANES + t:2 * SUBLANES + t, :] = jnp.where(blk == nb - 1, 0.0, nxt)


def _lru_kernel(*refs, t, nb, reverse, emit_y):
    if emit_y:
        (ux_ref, prev_ref, next_ref, gate_ref, hb_ref, cw_ref, cb_ref, wg_ref, bg_ref, lam_ref, h0_ref,
         out_ref, st_ref, ext_s, car_s) = refs
    else:
        (ux_ref, prev_ref, next_ref, cw_ref, cb_ref, wg_ref, bg_ref, lam_ref, h0_ref,
         out_ref, st_ref, ext_s, car_s) = refs
    i = pl.program_id(1)
    blk = (nb - 1 - i) if reverse else i
    w = GROUP_W
    ngrp = w // HEAD_W

    @pl.when(i == 0)
    def _():
        car_s[...] = h0_ref[0]

    _fill_ext(ext_s, ux_ref[...], prev_ref[...], next_ref[...], blk, nb, t)
    u = cb_ref[...] + cw_ref[0:1, :] * ext_s[pl.ds(SUBLANES - 2, t), :]
    for k in range(1, LRU_CONV):
        u = u + cw_ref[k:k + 1, :] * ext_s[pl.ds(SUBLANES - 2 + k, t), :]

    g = jnp.dot(u.astype(BF16), wg_ref[...], preferred_element_type=F32) + bg_ref[...]
    r = _sigmoid(g[:, :w])
    ig = _sigmoid(g[:, w:])
    nl = -lam_ref[...]
    softplus = jnp.maximum(nl, 0.0) + jnp.log1p(jnp.exp(-jnp.abs(nl)))
    log_a = (-LRU_C) * r * softplus
    a = jnp.exp(log_a)
    one_m = 1.0 - a * a
    bb = (one_m * lax.rsqrt(jnp.maximum(one_m, SQRT_GUARD))) * (ig * u)

    ng = t // SUBLANES
    rowid = lax.broadcasted_iota(jnp.int32, (1, SUBLANES, HEAD_W), 1)
    c_all = car_s[...]
    c_new, h_cols = [], []
    for cg in range(ngrp):
        av = a[:, cg * HEAD_W:(cg + 1) * HEAD_W].reshape(ng, SUBLANES, HEAD_W)
        bv = bb[:, cg * HEAD_W:(cg + 1) * HEAD_W].reshape(ng, SUBLANES, HEAD_W)
        for dist in (1, 2, 4):
            shift = SUBLANES - dist if reverse else dist
            valid = (rowid < SUBLANES - dist) if reverse else (rowid >= dist)
            b_far = jnp.where(valid, pltpu.roll(bv, shift, 1), 0.0)
            a_far = jnp.where(valid, pltpu.roll(av, shift, 1), 1.0)
            bv = bv + av * b_far
            av = av * a_far
        c = c_all[:, cg * HEAD_W:(cg + 1) * HEAD_W]
        hs = [None] * ng
        for g in (range(ng - 1, -1, -1) if reverse else range(ng)):
            hs[g] = bv[g] + av[g] * c
            c = hs[g][0:1, :] if reverse else hs[g][SUBLANES - 1:SUBLANES, :]
        c_new.append(c)
        h_cols.append(jnp.concatenate(hs, axis=0))
    c = jnp.concatenate(c_new, axis=1)
    car_s[...] = c
    st_ref[0] = jnp.broadcast_to(c, (SUBLANES, w))
    hfull = jnp.concatenate(h_cols, axis=1)
    if emit_y:
        out_ref[...] = (_gelu(gate_ref[...]) * (hfull + hb_ref[...])).astype(out_ref.dtype)
    else:
        out_ref[...] = hfull


def _lru(p_lru, hb, cw, cb, wg, bg, lam, h0, batch, seq, t, reverse):
    n = p_lru.shape[0]
    w = GROUP_W
    nb = seq // t
    emit_y = not reverse
    blk_of = (lambda i: nb - 1 - i) if reverse else (lambda i: i)
    main = lambda col: pl.BlockSpec((t, w), lambda b, i: (b * nb + blk_of(i), col))
    prev, nxt = _halo_specs(w, 0, n, seq, t, blk_of)
    const = lambda shape: pl.BlockSpec(shape, lambda b, i: (0,) * len(shape))
    in_specs = [main(0), prev, nxt]
    args = [p_lru, p_lru, p_lru]
    if emit_y:
        in_specs += [main(1), main(0)]
        args += [p_lru, hb]
    in_specs += [const((LRU_CONV, w)), const((1, w)), const((w, 2 * w)), const((1, 2 * w)), const((1, w)),
                 pl.BlockSpec((1, 1, w), lambda b, i: (b, 0, 0))]
    args += [cw, cb, wg, bg, lam, h0]
    scratch = [pltpu.VMEM((t + 2 * SUBLANES, w), F32), pltpu.VMEM((1, w), F32)]
    return pl.pallas_call(
        functools.partial(_lru_kernel, t=t, nb=nb, reverse=reverse, emit_y=emit_y),
        grid=(batch, nb),
        in_specs=in_specs,
        out_specs=[pl.BlockSpec((t, w), lambda b, i: (b * nb + blk_of(i), 0)),
                   pl.BlockSpec((1, SUBLANES, w), lambda b, i: (b, 0, 0))],
        out_shape=[jax.ShapeDtypeStruct((n, w), BF16 if emit_y else F32),
                   jax.ShapeDtypeStruct((batch, SUBLANES, w), F32)],
        scratch_shapes=scratch,
        compiler_params=_params("arbitrary", "arbitrary"),
        name="lru_fwd" if emit_y else "lru_bwd",
    )(*args)


def _pool_kernel(x_ref, prev_ref, next_ref, wp_ref, sp_ref, o_ref, ext_s, *, t, nb, seq):
    i = pl.program_id(1)
    x = x_ref[...]
    _fill_ext(ext_s, x, prev_ref[...], next_ref[...], i, nb, t)
    pos = i * t + lax.broadcasted_iota(jnp.int32, (t, HEAD_W), 0)
    for g, hw in enumerate(POOL_HALF):
        cols = slice(g * HEAD_W, (g + 1) * HEAD_W)
        s = ext_s[pl.ds(SUBLANES - hw, t), cols]
        for dlt in range(-hw + 1, hw):
            s = s + ext_s[pl.ds(SUBLANES + dlt, t), cols]
        cnt = (jnp.minimum(pos + hw - 1, seq - 1) - jnp.maximum(pos - hw, 0) + 1).astype(F32)
        dif = s / cnt - x[:, cols]
        y = jnp.dot(dif.astype(BF16), wp_ref[g], preferred_element_type=F32) * sp_ref[:, cols]
        o_ref[:, cols] = y.astype(o_ref.dtype)


def _pool(p_pool, wp_bf, sp, batch, seq, t):
    n, w = p_pool.shape
    nb = seq // t
    prev, nxt = _halo_specs(w, 0, n, seq, t, lambda i: i)
    return pl.pallas_call(
        functools.partial(_pool_kernel, t=t, nb=nb, seq=seq),
        grid=(batch, nb),
        in_specs=[pl.BlockSpec((t, w), lambda b, i: (b * nb + i, 0)), prev, nxt,
                  pl.BlockSpec(wp_bf.shape, lambda b, i: (0, 0, 0)),
                  pl.BlockSpec((1, w), lambda b, i: (0, 0))],
        out_specs=pl.BlockSpec((t, w), lambda b, i: (b * nb + i, 0)),
        out_shape=jax.ShapeDtypeStruct((n, w), BF16),
        scratch_shapes=[pltpu.VMEM((t + 2 * SUBLANES, w), F32)],
        compiler_params=_params("arbitrary", "arbitrary"),
        name="pool",
    )(p_pool, p_pool, p_pool, wp_bf, sp)


def _cos_sin(n):
    ang = 2.0 * np.pi * np.outer(np.arange(n), np.arange(n)) / n
    return np.cos(ang), np.sin(ang)


def _ffta_kernel(x_ref, f1_ref, twc_ref, tws_ref, wc_ref, o_ref, *, t1n):
    w = GROUP_W
    hh = jnp.dot(f1_ref[...], x_ref[0].astype(BF16), preferred_element_type=F32)
    hr, hi = hh[:DFT_N], hh[DFT_N:]
    twc, tws = twc_ref[0], tws_ref[0]
    parts_r, parts_i = [], []
    for t1 in range(t1n):
        cw, sw = twc[:, t1:t1 + 1], tws[:, t1:t1 + 1]
        ar, ai = hr[:, t1 * w:(t1 + 1) * w], hi[:, t1 * w:(t1 + 1) * w]
        br = ar * cw + ai * sw
        bi = ai * cw - ar * sw
        for h in range(HEADS):
            parts_r.append(br[:, h * HEAD_W:(h + 1) * HEAD_W])
            parts_i.append(bi[:, h * HEAD_W:(h + 1) * HEAD_W])
    ab = jnp.concatenate([jnp.concatenate(parts_r, axis=0), jnp.concatenate(parts_i, axis=0)], axis=1)
    g = jnp.dot(ab.astype(BF16), wc_ref[...], preferred_element_type=F32)
    for t1 in range(t1n):
        for h in range(HEADS):
            rows = slice((t1 * HEADS + h) * DFT_N, (t1 * HEADS + h + 1) * DFT_N)
            base = t1 * 2 * w + h * HEAD_W
            o_ref[0, :, base:base + HEAD_W] = g[rows, :HEAD_W].astype(o_ref.dtype)
            o_ref[0, :, base + w:base + w + HEAD_W] = g[rows, HEAD_W:].astype(o_ref.dtype)


def _fftb_kernel(g_ref, f3_ref, wf_ref, o_ref):
    w = GROUP_W
    for kk in range(g_ref.shape[1]):
        blk = g_ref[0, kk]
        gg = jnp.concatenate([blk[:, :w], blk[:, w:]], axis=0).astype(BF16)
        y = jnp.dot(f3_ref[...], gg, preferred_element_type=F32)
        o_ref[0, :, kk * w:(kk + 1) * w] = jnp.dot(y.astype(BF16), wf_ref[...],
                                                   preferred_element_type=F32).astype(o_ref.dtype)


def _fft_big(p_fft, wf_bf, batch, seq):
    w = GROUP_W
    n1 = seq // DFT_N
    t1n = SUBLANES
    c1, s1 = _cos_sin(DFT_N)
    cn, sn = _cos_sin(n1)
    sc = 1.0 / math.sqrt(DFT_N)
    f1 = jnp.asarray(np.concatenate([c1, -s1], axis=0) * sc, BF16)
    wc = jnp.asarray(np.block([[c1, -s1], [s1, c1]]) * sc, BF16)
    f3 = jnp.asarray(np.concatenate([cn, sn], axis=1) / math.sqrt(n1), BF16)
    ang = 2.0 * np.pi * np.outer(np.arange(DFT_N), np.arange(n1)) / seq
    tw = lambda m: jnp.asarray(m.reshape(DFT_N, n1 // t1n, t1n).transpose(1, 0, 2), F32)
    g = pl.pallas_call(
        functools.partial(_ffta_kernel, t1n=t1n),
        grid=(batch, n1 // t1n),
        in_specs=[pl.BlockSpec((1, DFT_N, t1n * w), lambda b, j: (b, 0, j)),
                  pl.BlockSpec(f1.shape, lambda b, j: (0, 0)),
                  pl.BlockSpec((1, DFT_N, t1n), lambda b, j: (j, 0, 0)),
                  pl.BlockSpec((1, DFT_N, t1n), lambda b, j: (j, 0, 0)),
                  pl.BlockSpec(wc.shape, lambda b, j: (0, 0))],
        out_specs=pl.BlockSpec((1, DFT_N, t1n * 2 * w), lambda b, j: (b, 0, j)),
        out_shape=jax.ShapeDtypeStruct((batch, DFT_N, n1 * 2 * w), BF16),
        compiler_params=_params("arbitrary", "arbitrary"),
        name="fft_a",
    )(p_fft.reshape(batch, DFT_N, n1 * w), f1, tw(np.cos(ang)), tw(np.sin(ang)), wc)
    y = pl.pallas_call(
        _fftb_kernel,
        grid=(batch, DFT_N // FFT_KB),
        in_specs=[pl.BlockSpec((1, FFT_KB, n1, 2 * w), lambda b, k: (b, k, 0, 0)),
                  pl.BlockSpec(f3.shape, lambda b, k: (0, 0)),
                  pl.BlockSpec((w, w), lambda b, k: (0, 0))],
        out_specs=pl.BlockSpec((1, n1, FFT_KB * w), lambda b, k: (b, 0, k)),
        out_shape=jax.ShapeDtypeStruct((batch, n1, DFT_N * w), BF16),
        compiler_params=_params("arbitrary", "arbitrary"),
        name="fft_b",
    )(g.reshape(batch, DFT_N, n1, 2 * w), f3, wf_bf)
    return y.reshape(batch * seq, w)


def _fftd_kernel(x_ref, cs_ref, fl_ref, wf_ref, o_ref):
    xb = x_ref[...].astype(BF16)
    pa, pb = [], []
    for h in range(HEADS):
        z = jnp.dot(xb[:, h * HEAD_W:(h + 1) * HEAD_W], cs_ref[...], preferred_element_type=F32)
        pa.append(z[:, :HEAD_W])
        pb.append(z[:, HEAD_W:])
    ab = jnp.concatenate([jnp.concatenate(pa, axis=1), jnp.concatenate(pb, axis=1)], axis=0).astype(BF16)
    y = jnp.dot(fl_ref[...], ab, preferred_element_type=F32)
    o_ref[...] = jnp.dot(y.astype(BF16), wf_ref[...], preferred_element_type=F32).astype(o_ref.dtype)


def _fft_dense(p_fft, wf_bf, batch, seq):
    w = GROUP_W
    c1, s1 = _cos_sin(DFT_N)
    cl, sl = _cos_sin(seq)
    cs = jnp.asarray(np.concatenate([c1, s1], axis=1) / math.sqrt(DFT_N), BF16)
    fl = jnp.asarray(np.concatenate([cl, -sl], axis=1) / math.sqrt(seq), BF16)
    return pl.pallas_call(
        _fftd_kernel,
        grid=(batch,),
        in_specs=[pl.BlockSpec((seq, w), lambda b: (b, 0)),
                  pl.BlockSpec(cs.shape, lambda b: (0, 0)),
                  pl.BlockSpec(fl.shape, lambda b: (0, 0)),
                  pl.BlockSpec((w, w), lambda b: (0, 0))],
        out_specs=pl.BlockSpec((seq, w), lambda b: (b, 0)),
        out_shape=jax.ShapeDtypeStruct((batch * seq, w), BF16),
        compiler_params=_params("arbitrary"),
        name="fft_dense",
    )(p_fft, cs, fl, wf_bf)


def _attn_kernel(*refs, n, n_pre, lam_init):
    if n_pre:
        qt_ref, k_ref, vt_ref, kp_ref, vtp_ref, dl_ref, gs_ref, o_ref, s_s, cm_s, acc_s = refs
    else:
        qt_ref, k_ref, vt_ref, dl_ref, gs_ref, o_ref, s_s, cm_s, acc_s = refs
        kp_ref = vtp_ref = None
    qt = qt_ref[0, 0]
    tq = qt.shape[1]
    row = lax.broadcasted_iota(jnp.int32, qt.shape, 0)
    zero = jnp.zeros_like(qt)
    qs = (jnp.where(row < ATT_DH, qt, zero), jnp.where(row >= ATT_DH, qt, zero))
    acc_s[...] = jnp.zeros_like(acc_s)

    tk = k_ref.shape[3]

    def stage_a(c, buf, width, pre=False):
        kr = kp_ref if pre else k_ref
        for j in range(2):
            cm = None
            for part in range(width):
                s = jnp.dot(kr[0, 0, c + part], qs[j], preferred_element_type=F32)
                s_s[buf, j, part * tk:(part + 1) * tk, :] = s
                top = jnp.max(s, axis=0, keepdims=True)
                cm = top if cm is None else jnp.maximum(cm, top)
            cm_s[buf, j] = cm

    def stage_b(c, buf, width, ms, pre=False):
        vr = vtp_ref if pre else vt_ref
        vtb = jnp.concatenate([vr[0, 0, c + part] for part in range(width)], axis=1)
        out = []
        for j in range(2):
            mn = jnp.maximum(ms[j], cm_s[buf, j])
            alpha = jnp.exp2(ms[j] - mn)
            p = jnp.exp2(s_s[buf, j, 0:width * tk, :] - mn)
            acc_s[j] = alpha * acc_s[j] + jnp.dot(vtb, p.astype(BF16), preferred_element_type=F32)
            out.append(mn)
        return tuple(out)

    neg = jnp.full((1, tq), -1e30, F32)
    base = n % ATT_WIDE
    nw = n // ATT_WIDE
    first = lambda w: base + ATT_WIDE * w
    order = ([(first(w), ATT_WIDE, False) for w in range(nw)] + [(c, 1, False) for c in range(base)]
             + [(c, 1, True) for c in range(n_pre)])

    def pair(i, ms):
        stage_a(first(2 * i + 1), 1, ATT_WIDE)
        ms = stage_b(first(2 * i), 0, ATT_WIDE, ms)
        stage_a(first(2 * i + 2), 0, ATT_WIDE)
        return stage_b(first(2 * i + 1), 1, ATT_WIDE, ms)

    stage_a(order[0][0], 0, order[0][1], order[0][2])
    trips = max(nw - 1, 0) // 2
    ms = (neg, neg)
    if trips > 0:
        ms = lax.fori_loop(0, trips, pair, ms, unroll=math.gcd(trips, ATT_UNROLL))
    tail = order[2 * trips:]
    for idx, (c, width, pre) in enumerate(tail):
        step = 2 * trips + idx
        if idx + 1 < len(tail):
            stage_a(tail[idx + 1][0], (step + 1) % 2, tail[idx + 1][1], tail[idx + 1][2])
        ms = stage_b(c, step % 2, width, ms, pre)
    dl = dl_ref[...]
    lam = (jnp.exp(jnp.sum(dl[0:1] * dl[1:2], axis=-1, keepdims=True))
           - jnp.exp(jnp.sum(dl[2:3] * dl[3:4], axis=-1, keepdims=True)) + lam_init)
    o0 = acc_s[0, 0:HEAD_W, :] / acc_s[0, HEAD_W:HEAD_W + 1, :]
    o1 = acc_s[1, 0:HEAD_W, :] / acc_s[1, HEAD_W:HEAD_W + 1, :]
    ot = o0 - lam * o1
    o_ref[...] = (_rms(ot.T, gs_ref[...]) * (1.0 - lam_init)).astype(o_ref.dtype)


def _attn(qt, k, vt, prefix, dl, gs, batch, seq, tq, lam_init):
    nq = seq // tq
    n, tk = k.shape[2], k.shape[3]
    kv_spec = lambda arr: pl.BlockSpec((1, 1) + arr.shape[2:], lambda b, h, i: (b, h, 0, 0, 0))
    kv = [k, vt] + (list(prefix) if prefix is not None else [])
    n_pre = prefix[0].shape[2] if prefix is not None else 0
    return pl.pallas_call(
        functools.partial(_attn_kernel, n=n, n_pre=n_pre, lam_init=lam_init),
        grid=(batch, HEADS, nq),
        in_specs=[pl.BlockSpec((1, 1, HEAD_W, tq), lambda b, h, i: (b, h, 0, i))]
                 + [kv_spec(arr) for arr in kv]
                 + [pl.BlockSpec(dl.shape, lambda b, h, i: (0, 0)),
                    pl.BlockSpec((1, HEAD_W), lambda b, h, i: (0, 0))],
        out_specs=pl.BlockSpec((tq, HEAD_W), lambda b, h, i: (b * nq + i, h)),
        out_shape=jax.ShapeDtypeStruct((batch * seq, GROUP_W), BF16),
        scratch_shapes=[pltpu.VMEM((2, 2, ATT_WIDE * tk, tq), F32), pltpu.VMEM((2, 2, 1, tq), F32),
                        pltpu.VMEM((2, VT_ROWS, tq), F32)],
        compiler_params=_params("arbitrary", "arbitrary", "arbitrary"),
        name="attn",
    )(qt, *kv, dl, gs)


def _outproj_kernel(yl_ref, yp_ref, yf_ref, ya_ref, w_ref, x_ref, g_ref, ng_ref, o_ref):
    w = GROUP_W
    y = jnp.dot(yl_ref[...], w_ref[0:w, :], preferred_element_type=F32)
    for j, r in enumerate((yp_ref, yf_ref, ya_ref), start=1):
        y = y + jnp.dot(r[...], w_ref[j * w:(j + 1) * w, :], preferred_element_type=F32)
    o_ref[...] = x_ref[...] + g_ref[0] * _rms(y, ng_ref[...])


def _out_proj(ys, w_all, layer_idx, x2d, modl, ng, tm, tiles_per_batch):
    n, d = x2d.shape
    ytile = pl.BlockSpec((tm, GROUP_W), lambda i: (i, 0))
    return pl.pallas_call(
        _outproj_kernel,
        grid=(n // tm,),
        in_specs=[ytile, ytile, ytile, ytile,
                  pl.BlockSpec((None,) + w_all.shape[1:], lambda i: (layer_idx, 0, 0),
                               pipeline_mode=pl.Buffered(1)),
                  pl.BlockSpec((tm, d), lambda i: (i, 0)),
                  _mod_spec(d, 2, tiles_per_batch),
                  pl.BlockSpec((1, d), lambda i: (0, 0))],
        out_specs=pl.BlockSpec((tm, d), lambda i: (i, 0)),
        out_shape=jax.ShapeDtypeStruct((n, d), F32),
        compiler_params=_params("arbitrary"),
        name="out_proj",
    )(*ys, w_all, x2d, modl, ng)


def _ffn_kernel(x_ref, prev_ref, next_ref, sh_ref, sc_ref, g2_ref, ng2_ref, ng3_ref, wg_ref, wv_ref, cw_ref,
                cb_ref, wd_ref, o_ref, hn_s, gs_s, *, tm, nb, nf):
    i = pl.program_id(0) % nb
    j = pl.program_id(1)
    hr = BF16_ROWS

    @pl.when(j == 0)
    def _():
        norm = lambda v: (_rms(v, ng2_ref[...]) * (1.0 + sc_ref[0]) + sh_ref[0]).astype(BF16)
        hn_s[0:hr, :] = norm(prev_ref[...])
        hn_s[hr:hr + tm, :] = norm(x_ref[...])
        hn_s[hr + tm:2 * hr + tm, :] = norm(next_ref[...])
        o_ref[...] = jnp.zeros_like(o_ref)

    g = jnp.dot(hn_s[...], wg_ref[...], preferred_element_type=F32)
    gs_s[0:hr, :] = jnp.where(i == 0, 0.0, g[0:hr])
    gs_s[hr:hr + tm, :] = g[hr:hr + tm]
    gs_s[hr + tm:2 * hr + tm, :] = jnp.where(i == nb - 1, 0.0, g[hr + tm:2 * hr + tm])
    gc = cb_ref[...] + cw_ref[0:1, :] * gs_s[pl.ds(hr - 1, tm), :]
    for k in range(1, FFN_CONV):
        gc = gc + cw_ref[k:k + 1, :] * gs_s[pl.ds(hr - 1 + k, tm), :]
    v = jnp.dot(hn_s[hr:hr + tm, :], wv_ref[...], preferred_element_type=F32)
    o_ref[...] += jnp.dot((_gelu(gc) * v).astype(BF16), wd_ref[...], preferred_element_type=F32)

    @pl.when(j == nf - 1)
    def _():
        o_ref[...] = x_ref[...] + g2_ref[0] * _rms(o_ref[...], ng3_ref[...])


def _ffn(x2d, modl, ng2, ng3, wup_all, wdn_all, layer_idx, cw, cb, seq, tm, tf, tiles_per_batch):
    n, d = x2d.shape
    d_ff = wdn_all.shape[1]
    nb, nf = seq // tm, d_ff // tf
    hr = BF16_ROWS
    per_t, last = tm // hr, n // hr - 1
    prev = pl.BlockSpec((hr, d), lambda i, j: (jnp.maximum(i * per_t - 1, 0), 0))
    nxt = pl.BlockSpec((hr, d), lambda i, j: (jnp.minimum((i + 1) * per_t, last), 0))
    vec = pl.BlockSpec((1, d), lambda i, j: (0, 0))
    return pl.pallas_call(
        functools.partial(_ffn_kernel, tm=tm, nb=nb, nf=nf),
        grid=(n // tm, nf),
        in_specs=[pl.BlockSpec((tm, d), lambda i, j: (i, 0)), prev, nxt,
                  _mod_spec(d, 3, tiles_per_batch), _mod_spec(d, 4, tiles_per_batch),
                  _mod_spec(d, 5, tiles_per_batch), vec, vec,
                  pl.BlockSpec((None, d, tf), lambda i, j: (layer_idx, 0, j)),
                  pl.BlockSpec((None, d, tf), lambda i, j: (layer_idx, 0, nf + j)),
                  pl.BlockSpec((FFN_CONV, tf), lambda i, j: (0, j)),
                  pl.BlockSpec((1, tf), lambda i, j: (0, j)),
                  pl.BlockSpec((None, tf, d), lambda i, j: (layer_idx, j, 0))],
        out_specs=pl.BlockSpec((tm, d), lambda i, j: (i, 0)),
        out_shape=jax.ShapeDtypeStruct((n, d), F32),
        scratch_shapes=[pltpu.VMEM((tm + 2 * hr, d), BF16), pltpu.VMEM((tm + 2 * hr, tf), F32)],
        compiler_params=_params("arbitrary", "arbitrary"),
        name="ffn",
    )(x2d, x2d, x2d, modl, modl, modl, ng2, ng3, wup_all, wup_all, cw, cb, wdn_all)


def _rope_tables(seq):
    t = jnp.arange(seq)
    row = (t // GRID_W).astype(F32)
    col = (t % GRID_W).astype(F32)
    inv = ROPE_BASE ** (-jnp.arange(N_FREQ, dtype=F32) / N_FREQ)
    ang_r, ang_c = row[:, None] * inv, col[:, None] * inv
    cos = jnp.concatenate([jnp.cos(ang_r)] * 2 + [jnp.cos(ang_c)] * 2, axis=1)
    sin = jnp.concatenate([-jnp.sin(ang_r), jnp.sin(ang_r), -jnp.sin(ang_c), jnp.sin(ang_c)], axis=1)
    return jnp.tile(cos, (1, 2)), jnp.tile(sin, (1, 2))


def _gate_weights(wa, ba, wi, bi):
    eye = jnp.eye(LRU_BLOCKS, dtype=wa.dtype)
    dense = lambda m: jnp.einsum('nde,nm->ndme', m, eye).reshape(GROUP_W, GROUP_W)
    return (jnp.concatenate([dense(wa), dense(wi)], axis=1).astype(BF16),
            jnp.concatenate([ba, bi])[None, :])


def _row_tile(seq):
    return min(seq, 512)


def _mixers(p, layer, batch, seq, h0f, h0b, prefix, lam_init, want_y):
    p_lru, p_pool, p_fft, qt, k, vt = p
    t = _row_tile(seq)
    hb, st_b = _lru(p_lru, None, layer['cw'], layer['cb'], layer['wg'][1], layer['bg'][1], layer['lam'][1],
                    h0b, batch, seq, t, True)
    y_lru, st_f = _lru(p_lru, hb, layer['cw'], layer['cb'], layer['wg'][0], layer['bg'][0], layer['lam'][0],
                       h0f, batch, seq, t, False)
    ys = None
    if want_y:
        y_pool = _pool(p_pool, layer['wp'], layer['sp'], batch, seq, t)
        if seq % (DFT_N * SUBLANES) == 0:
            y_fft = _fft_big(p_fft, layer['wf'], batch, seq)
        else:
            y_fft = _fft_dense(p_fft, layer['wf'], batch, seq)
        y_att = _attn(qt, k, vt, prefix, layer['dl'], layer['gs'], batch, seq, t, lam_init)
        ys = (y_lru, y_pool, y_fft, y_att)
    return ys, (st_f, st_b), (k, vt)


def kernel(x, c, ctx, c_ctx, ada_w, ada_b, norm_g, w_in, lru_conv_w, lru_conv_b, lru_wa, lru_ba, lru_wi, lru_bi,
           lru_lam, pool_w, pool_scale, fourier_w, diff_lam, diff_subln_g, w_out, ffn_w_up, ffn_conv_w,
           ffn_conv_b, ffn_w_down):
    batch, seq, d = x.shape
    ctx_len = ctx.shape[1]
    depth = ada_w.shape[0]
    d_ff = ffn_w_down.shape[1]
    tm, tmc = _row_tile(seq), _row_tile(ctx_len)
    tpb = seq // tm
    tf = 512

    c8 = jnp.concatenate([c.astype(F32), c_ctx.astype(F32)[None, :],
                          jnp.zeros((SUBLANES - batch - 1, d), F32)], axis=0)
    mod = _ada(c8, ada_w, ada_b).reshape(depth, SUBLANES * N_MOD, 1, d)
    rope = _rope_tables(seq)
    zeros_state = jnp.zeros((batch, 1, GROUP_W), F32)
    w_in_all, w_out_all = w_in.astype(BF16), w_out.astype(BF16)
    w_up_all, w_dn_all = ffn_w_up.astype(BF16), ffn_w_down.astype(BF16)

    x2 = x.reshape(batch * seq, d)
    xc = ctx.reshape(batch * ctx_len, d)
    for l in range(depth):
        last = l == depth - 1
        lam_init = 0.8 - 0.6 * math.exp(-0.3 * l)
        gates = [_gate_weights(lru_wa[l, dr], lru_ba[l, dr], lru_wi[l, dr], lru_bi[l, dr]) for dr in range(2)]
        layer = dict(cw=lru_conv_w[l], cb=lru_conv_b[l][None, :],
                     wg=[g[0] for g in gates], bg=[g[1] for g in gates],
                     lam=[lru_lam[l, dr][None, :] for dr in range(2)],
                     wp=pool_w[l].astype(BF16), sp=pool_scale[l][None, :], wf=fourier_w[l].astype(BF16),
                     dl=diff_lam[l], gs=diff_subln_g[l][None, :])
        ng = [norm_g[l, k][None, :] for k in range(4)]
        modl = mod[l]
        ffn = functools.partial(_ffn, ng2=ng[2], ng3=ng[3], wup_all=w_up_all, wdn_all=w_dn_all, layer_idx=l,
                                cw=ffn_conv_w[l], cb=ffn_conv_b[l][None, :], tf=tf)

        pc = _in_proj(xc, modl, ng[0], w_in_all, l, None, batch, ctx_len, tmc, True)
        ysc, (stf, stb), ctx_kv = _mixers(pc, layer, batch, ctx_len, zeros_state, zeros_state, None,
                                          lam_init, not last)
        pz = _in_proj(x2, modl, ng[0], w_in_all, l, rope, batch, seq, tm, False)
        ys, _, _ = _mixers(pz, layer, batch, seq, stf[:, :1], stb[:, :1], ctx_kv, lam_init, True)
        x2 = _out_proj(ys, w_out_all, l, x2, modl, ng[1], tm, tpb)
        x2 = ffn(x2, modl, seq=seq, tm=tm, tiles_per_batch=tpb)
        if not last:
            xc = _out_proj(ysc, w_out_all, l, xc, modl, ng[1], tmc, None)
            xc = ffn(xc, modl, seq=ctx_len, tm=tmc, tiles_per_batch=None)
    return x2.reshape(batch, seq, d)
```
